```python
import jax, jax.numpy as jnp
from jax import lax
import numpy as np

D_MODEL = 1024
BATCH = 2
SEQ = 8192
DEPTH = 4
DEC_BATCH = 128
DEC_SEQ = 1
PAST_LEN = 8192
PAGE_SIZE = 128

N_MIXERS = 2
N_ATTN_LAYERS = (DEPTH + N_MIXERS - 1) // N_MIXERS
N_RET_LAYERS = DEPTH // N_MIXERS
ATTN_HEAD_DIM = 64
ATTN_HEADS = D_MODEL // ATTN_HEAD_DIM
ATTN_KV_HEADS = 4
WINDOW = 128
ROPE_THETA = 10000.0
RET_KEY_DIM = 256
RET_HEADS = D_MODEL // RET_KEY_DIM
RET_VALUE_DIM = 2 * RET_KEY_DIM
RET_CHUNK = 128
D_FF = 256 * ((8 * D_MODEL // 3 + 255) // 256)
CONV_WIDTH = 3
LN_EPS = 1e-5
GN_EPS = 1e-5
DEEPNORM_ALPHA = (2.0 * DEPTH) ** 0.25
DEEPNORM_BETA = (8.0 * DEPTH) ** -0.25

kernel_name = 'swa_sink_retention_convffn_step'


def layer_norm(x, g, b):
    xf = x.astype(jnp.float32)
    mu = jnp.mean(xf, axis=-1, keepdims=True)
    var = jnp.mean(jnp.square(xf - mu), axis=-1, keepdims=True)
    return ((xf - mu) * lax.rsqrt(var + LN_EPS) * g + b).astype(x.dtype)


def rope(x, pos):
    half = x.shape[-1] // 2
    inv_freq = ROPE_THETA ** (-jnp.arange(half, dtype=jnp.float32) / half)
    ang = pos.astype(jnp.float32)[:, None] * inv_freq[None, :]
    cos = jnp.cos(ang)[:, None, :]
    sin = jnp.sin(ang)[:, None, :]
    xf = x.astype(jnp.float32)
    x1, x2 = xf[..., :half], xf[..., half:]
    return jnp.concatenate([x1 * cos - x2 * sin, x2 * cos + x1 * sin], axis=-1).astype(x.dtype)


def retention_rotate(x, pos):
    d = x.shape[-1]
    angle = 1.0 / (10000.0 ** jnp.linspace(0.0, 1.0, d // 2, dtype=jnp.float32))
    ang = pos.astype(jnp.float32)[:, None] * angle[None, :]
    cos = jnp.cos(ang)[:, None, :]
    sin = jnp.sin(ang)[:, None, :]
    x1, x2 = x[..., 0::2], x[..., 1::2]
    return jnp.stack([x1 * cos - x2 * sin, x2 * cos + x1 * sin], axis=-1).reshape(x.shape)


def sink_attention(q, k, v, qpos, kpos, sinks):
    lead = q.shape[:-3]
    tq = q.shape[-3]
    group = ATTN_HEADS // ATTN_KV_HEADS
    qg = q.reshape(*lead, tq, ATTN_KV_HEADS, group, ATTN_HEAD_DIM)
    s = jnp.einsum('...qkgd,...skd->...kgqs', qg, k).astype(jnp.float32) * (ATTN_HEAD_DIM ** -0.5)
    rel = qpos[..., :, None] - kpos[..., None, :]
    valid = (kpos[..., None, :] >= 0) & (rel >= 0) & (rel <= WINDOW)
    s = jnp.where(valid[..., None, None, :, :], s, -jnp.inf)
    sink = sinks.astype(jnp.float32).reshape(ATTN_KV_HEADS, group, 1, 1)
    m = jnp.maximum(jnp.max(s, axis=-1, keepdims=True), sink)
    p = jnp.exp(s - m)
    w = p / (jnp.sum(p, axis=-1, keepdims=True) + jnp.exp(sink - m))
    o = jnp.einsum('...kgqs,...skd->...qkgd', w.astype(v.dtype), v)
    return o.reshape(*lead, tq, ATTN_HEADS, ATTN_HEAD_DIM)


def attn_project(x, w_qkv):
    b, t, _ = x.shape
    qkv = x @ w_qkv
    nq = ATTN_HEADS * ATTN_HEAD_DIM
    nkv = ATTN_KV_HEADS * ATTN_HEAD_DIM
    q = qkv[..., :nq].reshape(b, t, ATTN_HEADS, ATTN_HEAD_DIM)
    k = qkv[..., nq:nq + nkv].reshape(b, t, ATTN_KV_HEADS, ATTN_HEAD_DIM)
    v = qkv[..., nq + nkv:].reshape(b, t, ATTN_KV_HEADS, ATTN_HEAD_DIM)
    return q, k, v


def swa_prompt(x, w_qkv, sinks, w_o):
    b, t, _ = x.shape
    nb = t // WINDOW
    pos = jnp.arange(t)
    q, k, v = attn_project(x, w_qkv)
    q = rope(q, pos)
    k = rope(k, pos)

    def band(z):
        zb = z.reshape(b, nb, WINDOW, ATTN_KV_HEADS, ATTN_HEAD_DIM)
        prev = jnp.concatenate([jnp.zeros_like(zb[:, :1]), zb[:, :-1]], axis=1)
        return jnp.concatenate([prev, zb], axis=2)

    qb = q.reshape(b, nb, WINDOW, ATTN_HEADS, ATTN_HEAD_DIM)
    qpos = pos.reshape(nb, WINDOW)
    kpos = qpos[:, :1] - WINDOW + jnp.arange(2 * WINDOW)[None, :]
    o = sink_attention(qb, band(k), band(v), qpos, kpos, sinks)
    out = o.reshape(b, t, ATTN_HEADS * ATTN_HEAD_DIM) @ w_o
    return out, k[:, t - WINDOW:], v[:, t - WINDOW:]


def swa_sample(x, k_buf, v_buf, w_qkv, sinks, w_o):
    b, t, _ = x.shape
    pos = PAST_LEN + jnp.arange(t)
    q, k, v = attn_project(x, w_qkv)
    q = rope(q, pos)
    k = rope(k, pos)
    kc = jnp.concatenate([k_buf.astype(k.dtype), k], axis=1)
    vc = jnp.concatenate([v_buf.astype(v.dtype), v], axis=1)
    kpos = PAST_LEN - WINDOW + jnp.arange(WINDOW + t)
    o = sink_attention(q, kc, vc, pos, kpos, sinks)
    out = o.reshape(b, t, ATTN_HEADS * ATTN_HEAD_DIM) @ w_o
    return out, kc[:, t:], vc[:, t:]


def retention_chunkwise(q, k, v, s0, chunk):
    b, t, h, dk = q.shape
    dv = v.shape[-1]
    nc = t // chunk
    log_gamma = jnp.log(1.0 - 2.0 ** (-5.0 - jnp.arange(RET_HEADS, dtype=jnp.float32)))
    idx = jnp.arange(chunk, dtype=jnp.float32)
    rel = idx[:, None] - idx[None, :]
    decay_in = jnp.where(rel[None] >= 0, jnp.exp(log_gamma[:, None, None] * jnp.maximum(rel, 0.0)[None]), 0.0)
    q_decay = jnp.exp(log_gamma[None, :] * (idx[:, None] + 1.0))
    k_decay = jnp.exp(log_gamma[None, :] * (chunk - 1.0 - idx[:, None]))
    chunk_decay = jnp.exp(log_gamma * chunk)
    qc = jnp.moveaxis(q.reshape(b, nc, chunk, h, dk), 1, 0)
    kc = jnp.moveaxis(k.reshape(b, nc, chunk, h, dk), 1, 0)
    vc = jnp.moveaxis(v.reshape(b, nc, chunk, h, dv), 1, 0)

    def step(s, inp):
        qi, ki, vi = inp
        inner = jnp.einsum('bihd,bjhd->bhij', qi, ki) * decay_in
        o = jnp.einsum('bhij,bjhe->bihe', inner, vi)
        o = o + jnp.einsum('bihd,bhde->bihe', qi, s) * q_decay[None, :, :, None]
        s = s * chunk_decay[None, :, None, None] + jnp.einsum('bjhd,bjhe->bhde', ki * k_decay[None, :, :, None], vi)
        return s, o

    s_new, o = lax.scan(step, s0, (qc, kc, vc))
    return jnp.moveaxis(o, 0, 1).reshape(b, t, h, dv), s_new


def retention_mixer(x, s0, pos, chunk, w_in, w_o):
    b, t, _ = x.shape
    hcat = x @ w_in
    nqk = RET_HEADS * RET_KEY_DIM
    nv = RET_HEADS * RET_VALUE_DIM
    q = hcat[..., :nqk].reshape(b, t, RET_HEADS, RET_KEY_DIM).astype(jnp.float32)
    k = hcat[..., nqk:2 * nqk].reshape(b, t, RET_HEADS, RET_KEY_DIM).astype(jnp.float32)
    v = hcat[..., 2 * nqk:2 * nqk + nv].reshape(b, t, RET_HEADS, RET_VALUE_DIM).astype(jnp.float32)
    g = hcat[..., 2 * nqk + nv:].astype(jnp.float32)
    q = retention_rotate(q, pos)
    k = retention_rotate(k, pos) * (RET_KEY_DIM ** -0.5)
    o, s_new = retention_chunkwise(q, k, v, s0.astype(jnp.float32), chunk)
    mu = jnp.mean(o, axis=-1, keepdims=True)
    var = jnp.mean(jnp.square(o - mu), axis=-1, keepdims=True)
    o = ((o - mu) * lax.rsqrt(var + GN_EPS)).reshape(b, t, nv)
    out = (jax.nn.silu(g) * o).astype(x.dtype) @ w_o
    return out, s_new


def conv_ffn(x, buf, w_in, conv_w, conv_b, w_out):
    t = x.shape[1]
    u = x @ w_in
    a, g = u[..., :D_FF], u[..., D_FF:]
    ext = jnp.concatenate([buf.astype(a.dtype), a], axis=1)
    c = conv_b + ext[:, 0:t] * conv_w[0]
    for j in range(1, CONV_WIDTH):
        c = c + ext[:, j:j + t] * conv_w[j]
    h = jax.nn.silu(c) * g
    return h @ w_out, ext[:, t:]


def setup_inputs(seed: int = 0) -> dict:
    key = jax.random.key(seed)
    ks = jax.random.split(key, 20)

    def nrm(k, shape, scale):
        return jax.random.normal(k, shape, jnp.float32) * scale

    qkv_dim = (ATTN_HEADS + 2 * ATTN_KV_HEADS) * ATTN_HEAD_DIM
    ret_in_dim = 2 * RET_HEADS * RET_KEY_DIM + 2 * RET_HEADS * RET_VALUE_DIM
    return {
        'x_prompt': nrm(ks[0], (BATCH, SEQ, D_MODEL), 1.0),
        'x_sample': nrm(ks[1], (DEC_BATCH, DEC_SEQ, D_MODEL), 1.0),
        'cache_k_win': nrm(ks[2], (N_ATTN_LAYERS, DEC_BATCH, WINDOW, ATTN_KV_HEADS, ATTN_HEAD_DIM), 1.0),
        'cache_v_win': nrm(ks[3], (N_ATTN_LAYERS, DEC_BATCH, WINDOW, ATTN_KV_HEADS, ATTN_HEAD_DIM), 1.0),
        'state_ret': nrm(ks[4], (N_RET_LAYERS, DEC_BATCH, RET_HEADS, RET_KEY_DIM, RET_VALUE_DIM), 0.1),
        'state_conv': nrm(ks[5], (DEPTH, DEC_BATCH, CONV_WIDTH - 1, D_FF), 1.0),
        'attn_w_qkv': nrm(ks[6], (N_ATTN_LAYERS, D_MODEL, qkv_dim), D_MODEL ** -0.5),
        'attn_sinks': nrm(ks[7], (N_ATTN_LAYERS, ATTN_HEADS), 0.5),
        'attn_w_o': nrm(ks[8], (N_ATTN_LAYERS, ATTN_HEADS * ATTN_HEAD_DIM, D_MODEL), DEEPNORM_BETA * (ATTN_HEADS * ATTN_HEAD_DIM) ** -0.5),
        'ret_w_in': nrm(ks[9], (N_RET_LAYERS, D_MODEL, ret_in_dim), D_MODEL ** -0.5),
        'ret_w_o': nrm(ks[10], (N_RET_LAYERS, RET_HEADS * RET_VALUE_DIM, D_MODEL), DEEPNORM_BETA * (RET_HEADS * RET_VALUE_DIM) ** -0.5),
        'ffn_w_in': nrm(ks[11], (DEPTH, D_MODEL, 2 * D_FF), D_MODEL ** -0.5),
        'ffn_conv_w': nrm(ks[12], (DEPTH, CONV_WIDTH, D_FF), CONV_WIDTH ** -0.5),
        'ffn_conv_b': nrm(ks[13], (DEPTH, D_FF), 0.02),
        'ffn_w_out': nrm(ks[14], (DEPTH, D_FF, D_MODEL), DEEPNORM_BETA * D_FF ** -0.5),
        'ln_mix_g': 1.0 + nrm(ks[15], (DEPTH, D_MODEL), 0.02),
        'ln_mix_b': nrm(ks[16], (DEPTH, D_MODEL), 0.02),
        'ln_ffn_g': 1.0 + nrm(ks[17], (DEPTH, D_MODEL), 0.02),
        'ln_ffn_b': nrm(ks[18], (DEPTH, D_MODEL), 0.02),
    }


def reference(x_prompt, x_sample, cache_k_win, cache_v_win, state_ret, state_conv,
              attn_w_qkv, attn_sinks, attn_w_o, ret_w_in, ret_w_o,
              ffn_w_in, ffn_conv_w, ffn_conv_b, ffn_w_out,
              ln_mix_g, ln_mix_b, ln_ffn_g, ln_ffn_b):
    xp, xs = x_prompt, x_sample
    bp, tp = xp.shape[0], xp.shape[1]
    ts = xs.shape[1]
    pos_p = jnp.arange(tp)
    pos_s = PAST_LEN + jnp.arange(ts)
    kp_l, vp_l, ks_l, vs_l, rp_l, rs_l, cp_l, cs_l = [], [], [], [], [], [], [], []
    for i in range(DEPTH):
        j = i // N_MIXERS
        if i % N_MIXERS == 0:
            mp, kp, vp = swa_prompt(xp, attn_w_qkv[j], attn_sinks[j], attn_w_o[j])
            ms, ksm, vsm = swa_sample(xs, cache_k_win[j], cache_v_win[j], attn_w_qkv[j], attn_sinks[j], attn_w_o[j])
            kp_l.append(kp)
            vp_l.append(vp)
            ks_l.append(ksm)
            vs_l.append(vsm)
        else:
            s0 = jnp.zeros((bp, RET_HEADS, RET_KEY_DIM, RET_VALUE_DIM), jnp.float32)
            mp, rp = retention_mixer(xp, s0, pos_p, RET_CHUNK, ret_w_in[j], ret_w_o[j])
            ms, rs = retention_mixer(xs, state_ret[j], pos_s, ts, ret_w_in[j], ret_w_o[j])
            rp_l.append(rp)
            rs_l.append(rs)
        xp = layer_norm(DEEPNORM_ALPHA * xp + mp, ln_mix_g[i], ln_mix_b[i])
        xs = layer_norm(DEEPNORM_ALPHA * xs + ms, ln_mix_g[i], ln_mix_b[i])
        buf0 = jnp.zeros((bp, CONV_WIDTH - 1, D_FF), xp.dtype)
        fp, cp = conv_ffn(xp, buf0, ffn_w_in[i], ffn_conv_w[i], ffn_conv_b[i], ffn_w_out[i])
        fs, cs = conv_ffn(xs, state_conv[i], ffn_w_in[i], ffn_conv_w[i], ffn_conv_b[i], ffn_w_out[i])
        cp_l.append(cp)
        cs_l.append(cs)
        xp = layer_norm(DEEPNORM_ALPHA * xp + fp, ln_ffn_g[i], ln_ffn_b[i])
        xs = layer_norm(DEEPNORM_ALPHA * xs + fs, ln_ffn_g[i], ln_ffn_b[i])
    return (xp, xs,
            jnp.stack(kp_l), jnp.stack(vp_l), jnp.stack(rp_l), jnp.stack(cp_l),
            jnp.stack(ks_l), jnp.stack(vs_l), jnp.stack(rs_l), jnp.stack(cs_l))
```

```python
import functools

import jax
import jax.numpy as jnp
from jax import lax
from jax.experimental import pallas as pl
from jax.experimental.pallas import tpu as pltpu

F32 = jnp.float32
BF16 = jnp.bfloat16

D_MODEL = 1024
DEPTH = 4
PAST_LEN = 8192
N_MIXERS = 2
ATTN_HEAD_DIM = 64
ATTN_HEADS = 16
ATTN_KV_HEADS = 4
ATTN_GROUP = ATTN_HEADS // ATTN_KV_HEADS
WINDOW = 128
ROPE_THETA = 10000.0
RET_KEY_DIM = 256
RET_HEADS = 4
RET_VALUE_DIM = 512
RET_CHUNK = 128
D_FF = 2816
CONV_WIDTH = 3
LN_EPS = 1e-5
GN_EPS = 1e-5
DEEPNORM_ALPHA = (2.0 * DEPTH) ** 0.25
ATTN_SCALE = ATTN_HEAD_DIM ** -0.5
RET_K_SCALE = RET_KEY_DIM ** -0.5

V7X_LANES = 128
V7X_SUBLANES = 8
V7X_VMEM_LIMIT_BYTES = 56 * 1024 * 1024

NQ = ATTN_HEADS * ATTN_HEAD_DIM
NKV = ATTN_KV_HEADS * ATTN_HEAD_DIM
RET_NQK = RET_HEADS * RET_KEY_DIM
RET_NV = RET_HEADS * RET_VALUE_DIM
FFN_CHUNK = 256
FFN_NCHUNK = D_FF // FFN_CHUNK
MASKED_SCORE = -1e30


def _params(*semantics):
    return pltpu.CompilerParams(
        dimension_semantics=semantics, vmem_limit_bytes=V7X_VMEM_LIMIT_BYTES
    )


def _full(shape):
    n = len(shape)
    return pl.BlockSpec(shape, lambda *_: (0,) * n)


def _deepnorm_ln(x, m, g, b):
    y = DEEPNORM_ALPHA * x + m
    mu = jnp.mean(y, axis=-1, keepdims=True)
    d = y - mu
    var = jnp.mean(d * d, axis=-1, keepdims=True)
    return d * lax.rsqrt(var + LN_EPS) * g + b


def _silu(z):
    return z * jax.nn.sigmoid(z)


def _rope_tables(pos):
    half = ATTN_HEAD_DIM // 2
    inv_freq = ROPE_THETA ** (-jnp.arange(half, dtype=F32) / half)
    ang = pos.astype(F32)[:, None] * inv_freq[None, :]
    cos, sin = jnp.cos(ang), jnp.sin(ang)
    return jnp.tile(cos, (1, 4)), jnp.concatenate([-sin, sin, -sin, sin], axis=1)


def _ret_rot_tables(pos):
    d = RET_KEY_DIM
    angle = 1.0 / (10000.0 ** jnp.linspace(0.0, 1.0, d // 2, dtype=F32))
    ang = pos.astype(F32)[:, None] * angle[None, :]
    cos, sin = jnp.cos(ang), jnp.sin(ang)
    cos_t = jnp.repeat(cos, 2, axis=1)
    sin_t = jnp.stack([-sin, sin], axis=-1).reshape(pos.shape[0], d)
    return cos_t, sin_t


def _ret_log_gamma():
    return jnp.log(1.0 - 2.0 ** (-5.0 - jnp.arange(RET_HEADS, dtype=F32)))


def _ret_decay_tables(chunk):
    log_gamma = _ret_log_gamma()
    idx = jnp.arange(chunk, dtype=F32)
    rel = idx[:, None] - idx[None, :]
    decay_in = jnp.where(
        rel[None] >= 0,
        jnp.exp(log_gamma[:, None, None] * jnp.maximum(rel, 0.0)[None]),
        0.0,
    )
    q_decay = jnp.exp(log_gamma[:, None] * (idx[None, :] + 1.0))
    k_decay = jnp.exp(log_gamma[:, None] * (chunk - 1.0 - idx[None, :]))
    chunk_decay = jnp.exp(log_gamma * chunk)
    lane_bcast = lambda t: jnp.broadcast_to(t[:, :, None], (RET_HEADS, chunk, V7X_LANES))
    return decay_in, lane_bcast(q_decay), lane_bcast(k_decay), chunk_decay


def _rope_128(z, cos, sin_signed, first_half):
    partner = jnp.where(first_half, pltpu.roll(z, 96, 1), pltpu.roll(z, 32, 1))
    return z * cos + partner * sin_signed


def _pair_rot_128(z, cos, sin_signed, even_lane):
    partner = jnp.where(even_lane, pltpu.roll(z, 127, 1), pltpu.roll(z, 1, 1))
    return z * cos + partner * sin_signed


def _attn_qkv_kernel(x_ref, w_ref, cos_ref, sin_ref, q_ref, k2_ref, v2_ref, kf_ref, vf_ref, *, tm):
    i = pl.program_id(1)
    last = pl.num_programs(1) - 1
    xb = x_ref[...].astype(BF16)
    cos = cos_ref[...]
    sin = sin_ref[...]
    lane = lax.broadcasted_iota(jnp.int32, (tm, V7X_LANES), 1)
    first_half = (lane & 63) < 32
    lo = lane < 64

    def dup(z, take_low):
        zr = pltpu.roll(z, 64, 1)
        return jnp.where(lo, z, zr) if take_low else jnp.where(lo, zr, z)

    for j in range(NQ // 512):
        z = jnp.dot(xb, w_ref[:, j * 512:(j + 1) * 512], preferred_element_type=F32)
        for c in range(4):
            zc = _rope_128(z[:, c * 128:(c + 1) * 128], cos, sin, first_half) * ATTN_SCALE
            q_ref[:, j * 512 + c * 128:j * 512 + (c + 1) * 128] = zc.astype(BF16)
    zkv = jnp.dot(xb, w_ref[:, NQ:NQ + 2 * NKV], preferred_element_type=F32)
    for c in range(NKV // 128):
        kc = _rope_128(zkv[:, c * 128:(c + 1) * 128], cos, sin, first_half)
        vc = zkv[:, NKV + c * 128:NKV + (c + 1) * 128]
        for gg in range(2):
            g = 2 * c + gg
            k2_ref[:, g * 128:(g + 1) * 128] = dup(kc, gg == 0).astype(BF16)
            v2_ref[:, g * 128:(g + 1) * 128] = dup(vc, gg == 0).astype(BF16)

        @pl.when(i == last)
        def _():
            kf_ref[:, c * 128:(c + 1) * 128] = kc[tm - WINDOW:, :]
            vf_ref[:, c * 128:(c + 1) * 128] = vc[tm - WINDOW:, :]


def _attn_qkv(x, w_qkv_bf16, cos, sin, *, tm):
    b, t, _ = x.shape
    grid = (b, t // tm)
    tok = lambda w: pl.BlockSpec((None, tm, w), lambda bi, i: (bi, i, 0))
    tab = pl.BlockSpec((tm, V7X_LANES), lambda bi, i: (i, 0))
    tail = pl.BlockSpec((None, WINDOW, NKV), lambda bi, i: (bi, 0, 0))
    return pl.pallas_call(
        functools.partial(_attn_qkv_kernel, tm=tm),
        grid=grid,
        in_specs=[tok(D_MODEL), _full(w_qkv_bf16.shape), tab, tab],
        out_specs=[tok(NQ), tok(2 * NKV), tok(2 * NKV), tail, tail],
        out_shape=[
            jax.ShapeDtypeStruct((b, t, NQ), BF16),
            jax.ShapeDtypeStruct((b, t, 2 * NKV), BF16),
            jax.ShapeDtypeStruct((b, t, 2 * NKV), BF16),
            jax.ShapeDtypeStruct((b, WINDOW, NKV), F32),
            jax.ShapeDtypeStruct((b, WINDOW, NKV), F32),
        ],
        compiler_params=_params("arbitrary", "arbitrary"),
        name="attn_qkv",
    )(x, w_qkv_bf16, cos, sin)


def _attn_core_kernel(sink_ref, q_ref, kc_ref, kp_ref, vc_ref, vp_ref, x_ref, wo_ref, g_ref, b_ref,
                      o_ref, kcat, vcat, oscr, *, tq):
    i = pl.program_id(1)
    kcat[0:WINDOW, :] = kp_ref[...]
    kcat[WINDOW:, :] = kc_ref[...]
    vcat[0:WINDOW, :] = vp_ref[...]
    vcat[WINDOW:, :] = vc_ref[...]
    w = WINDOW
    ii = lax.broadcasted_iota(jnp.int32, (2 * w, 2 * w), 0) & (w - 1)
    jj = lax.broadcasted_iota(jnp.int32, (2 * w, 2 * w), 1)
    band = (jj >= ii) & (jj <= ii + w)
    first_key = jnp.where(i == 0, w, 0)
    lo = lax.broadcasted_iota(jnp.int32, (w, V7X_LANES), 1) < 64
    top = lax.broadcasted_iota(jnp.int32, (2 * w, 1), 0) < w
    zero = jnp.zeros((w, V7X_LANES), BF16)
    for qb in range(tq // w):
        mask = band & (jj >= first_key) if qb == 0 else band
        rows = slice(qb * w, (qb + 1) * w)
        keys = slice(qb * w, (qb + 2) * w)
        for p in range(ATTN_HEADS // 2):
            g = p // (ATTN_GROUP // 2)
            cols = slice(p * 128, (p + 1) * 128)
            gcols = slice(g * 128, (g + 1) * 128)
            q2 = q_ref[rows, cols]
            qs = jnp.concatenate([jnp.where(lo, q2, zero), jnp.where(lo, zero, q2)], axis=0)
            s = lax.dot_general(qs, kcat[keys, gcols], (((1,), (1,)), ((), ())),
                                preferred_element_type=F32)
            s = jnp.where(mask, s, MASKED_SCORE)
            sink = jnp.where(top, sink_ref[2 * p], sink_ref[2 * p + 1])
            m = jnp.maximum(jnp.max(s, axis=-1, keepdims=True), sink)
            e = jnp.exp(s - m)
            denom = jnp.sum(e, axis=-1, keepdims=True) + jnp.exp(sink - m)
            o2 = jnp.dot(e.astype(BF16), vcat[keys, gcols], preferred_element_type=F32)
            o2 = o2 / denom
            oscr[rows, cols] = jnp.where(lo, o2[:w], o2[w:]).astype(BF16)
    out = jnp.dot(oscr[...], wo_ref[...], preferred_element_type=F32)
    o_ref[...] = _deepnorm_ln(x_ref[...], out, g_ref[...], b_ref[...])


def _attn_core(sinks, q, k2, v2, x, wo_bf16, g, bta, *, tq):
    b, t, _ = x.shape
    grid = (b, t // tq)
    r = tq // WINDOW
    cur = lambda w: pl.BlockSpec((None, tq, w), lambda bi, i: (bi, i, 0))
    prev = pl.BlockSpec((None, WINDOW, 2 * NKV), lambda bi, i: (bi, jnp.maximum(i * r - 1, 0), 0))
    return pl.pallas_call(
        functools.partial(_attn_core_kernel, tq=tq),
        grid=grid,
        in_specs=[
            pl.BlockSpec(memory_space=pltpu.SMEM),
            cur(NQ), cur(2 * NKV), prev, cur(2 * NKV), prev, cur(D_MODEL),
            _full(wo_bf16.shape), _full(g.shape), _full(bta.shape),
        ],
        out_specs=cur(D_MODEL),
        out_shape=jax.ShapeDtypeStruct((b, t, D_MODEL), F32),
        scratch_shapes=[
            pltpu.VMEM((tq + WINDOW, 2 * NKV), BF16),
            pltpu.VMEM((tq + WINDOW, 2 * NKV), BF16),
            pltpu.VMEM((tq, NQ), BF16),
        ],
        compiler_params=_params("arbitrary", "arbitrary"),
        name="attn_core",
    )(sinks, q, k2, k2, v2, v2, x, wo_bf16, g, bta)


def _ret_in_kernel(x_ref, w_ref, cos_ref, sin_ref, q_ref, k_ref, v_ref, sg_ref, *, tm):
    xb = x_ref[...].astype(BF16)
    lane = lax.broadcasted_iota(jnp.int32, (tm, V7X_LANES), 1)
    even = (lane & 1) == 0
    nc = 512
    for sec, out_ref, scale in ((0, q_ref, None), (1, k_ref, RET_K_SCALE)):
        for j in range(RET_NQK // nc):
            z = jnp.dot(xb, w_ref[:, sec * RET_NQK + j * nc:sec * RET_NQK + (j + 1) * nc],
                        preferred_element_type=F32)
            for c in range(nc // 128):
                t0 = (c % 2) * 128
                zc = _pair_rot_128(z[:, c * 128:(c + 1) * 128], cos_ref[:, t0:t0 + 128],
                                   sin_ref[:, t0:t0 + 128], even)
                if scale is not None:
                    zc = zc * scale
                out_ref[:, j * nc + c * 128:j * nc + (c + 1) * 128] = zc.astype(BF16)
    base = 2 * RET_NQK
    for j in range(RET_NV // nc):
        z = jnp.dot(xb, w_ref[:, base + j * nc:base + (j + 1) * nc], preferred_element_type=F32)
        v_ref[:, j * nc:(j + 1) * nc] = z.astype(BF16)
    base = 2 * RET_NQK + RET_NV
    for j in range(RET_NV // nc):
        z = jnp.dot(xb, w_ref[:, base + j * nc:base + (j + 1) * nc], preferred_element_type=F32)
        sg_ref[:, j * nc:(j + 1) * nc] = _silu(z).astype(BF16)


def _ret_in(x, w_in_bf16, cos, sin, *, tm):
    b, t, _ = x.shape
    grid = (b, t // tm)
    tok = lambda w: pl.BlockSpec((None, tm, w), lambda bi, i: (bi, i, 0))
    tab = pl.BlockSpec((tm, RET_KEY_DIM), lambda bi, i: (i, 0))
    return pl.pallas_call(
        functools.partial(_ret_in_kernel, tm=tm),
        grid=grid,
        in_specs=[tok(D_MODEL), _full(w_in_bf16.shape), tab, tab],
        out_specs=[tok(RET_NQK), tok(RET_NQK), tok(RET_NV), tok(RET_NV)],
        out_shape=[
            jax.ShapeDtypeStruct((b, t, RET_NQK), BF16),
            jax.ShapeDtypeStruct((b, t, RET_NQK), BF16),
            jax.ShapeDtypeStruct((b, t, RET_NV), BF16),
            jax.ShapeDtypeStruct((b, t, RET_NV), BF16),
        ],
        compiler_params=_params("arbitrary", "arbitrary"),
        name="ret_in",
    )(x, w_in_bf16, cos, sin)


def _ret_core_kernel(cd_ref, q_ref, k_ref, v_ref, sg_ref, x_ref, din_ref, qd_ref, kd_ref, wo_ref,
                     g_ref, b_ref, o_ref, st_ref, s_scr, *, ct):
    i = pl.program_id(1)
    last = pl.num_programs(1) - 1
    c = RET_CHUNK

    @pl.when(i == 0)
    def _():
        s_scr[...] = jnp.zeros_like(s_scr)

    def chunk(ci, carry):
        r0 = pl.multiple_of(ci * c, c)
        rows = pl.ds(r0, c)
        acc = jnp.zeros((c, D_MODEL), F32)
        for h in range(RET_HEADS):
            kcols = slice(h * RET_KEY_DIM, (h + 1) * RET_KEY_DIM)
            vcols = slice(h * RET_VALUE_DIM, (h + 1) * RET_VALUE_DIM)
            qh = q_ref[rows, kcols]
            kh = k_ref[rows, kcols]
            vh = v_ref[rows, vcols]
            inner = lax.dot_general(qh, kh, (((1,), (1,)), ((), ())), preferred_element_type=F32)
            inner = inner * din_ref[h]
            s_h = s_scr[h]
            qd = jnp.concatenate([qd_ref[h]] * (RET_VALUE_DIM // V7X_LANES), axis=1)
            o = jnp.dot(inner.astype(BF16), vh, preferred_element_type=F32)
            o = o + jnp.dot(qh, s_h.astype(BF16), preferred_element_type=F32) * qd
            kd = jnp.concatenate([kd_ref[h]] * (RET_KEY_DIM // V7X_LANES), axis=1)
            kdec = (kh.astype(F32) * kd).astype(BF16)
            s_scr[h] = s_h * cd_ref[h] + lax.dot_general(
                kdec, vh, (((0,), (0,)), ((), ())), preferred_element_type=F32)
            mu = jnp.mean(o, axis=-1, keepdims=True)
            d = o - mu
            var = jnp.mean(d * d, axis=-1, keepdims=True)
            on = d * lax.rsqrt(var + GN_EPS)
            gated = (sg_ref[rows, vcols].astype(F32) * on).astype(BF16)
            acc = acc + jnp.dot(gated, wo_ref[vcols, :], preferred_element_type=F32)
        o_ref[rows, :] = _deepnorm_ln(x_ref[rows, :], acc, g_ref[...], b_ref[...])
        return carry

    lax.fori_loop(0, ct // c, chunk, 0)

    @pl.when(i == last)
    def _():
        st_ref[...] = s_scr[...]


def _ret_core(chunk_decay, q, k, v, sg, x, decay_in, q_decay, k_decay, wo_bf16, g, bta, *, ct):
    b, t, _ = x.shape
    grid = (b, t // ct)
    tok = lambda w: pl.BlockSpec((None, ct, w), lambda bi, i: (bi, i, 0))
    st_shape = (RET_HEADS, RET_KEY_DIM, RET_VALUE_DIM)
    return pl.pallas_call(
        functools.partial(_ret_core_kernel, ct=ct),
        grid=grid,
        in_specs=[
            pl.BlockSpec(memory_space=pltpu.SMEM),
            tok(RET_NQK), tok(RET_NQK), tok(RET_NV), tok(RET_NV), tok(D_MODEL),
            _full(decay_in.shape), _full(q_decay.shape), _full(k_decay.shape),
            _full(wo_bf16.shape), _full(g.shape), _full(bta.shape),
        ],
        out_specs=[tok(D_MODEL), pl.BlockSpec((None,) + st_shape, lambda bi, i: (bi, 0, 0, 0))],
        out_shape=[
            jax.ShapeDtypeStruct((b, t, D_MODEL), F32),
            jax.ShapeDtypeStruct((b,) + st_shape, F32),
        ],
        scratch_shapes=[pltpu.VMEM(st_shape, F32)],
        compiler_params=_params("arbitrary", "arbitrary"),
        name="ret_core",
    )(chunk_decay, q, k, v, sg, x, decay_in, q_decay, k_decay, wo_bf16, g, bta)


def _conv_gate(a, gt, s1, s2, cw):
    c = cw[3:4, :] + s2 * cw[0:1, :]
    c = c + s1 * cw[1:2, :]
    c = c + a * cw[2:3, :]
    return _silu(c) * gt


def _ffn_kernel(x_ref, wa_ref, wg_ref, cw_ref, wout_ref, g_ref, b_ref, o_ref, cs_ref,
                acc_ref, carry_ref, *, tm):
    i = pl.program_id(1)
    last = pl.num_programs(1) - 1

    @pl.when(i == 0)
    def _():
        carry_ref[...] = jnp.zeros_like(carry_ref)

    xb = x_ref[...].astype(BF16)
    row = lax.broadcasted_iota(jnp.int32, (tm, FFN_CHUNK), 0)
    acc_ref[...] = jnp.zeros_like(acc_ref)

    def body(j, carry):
        a = jnp.dot(xb, wa_ref[j], preferred_element_type=F32)
        gt = jnp.dot(xb, wg_ref[j], preferred_element_type=F32)
        prev = carry_ref[j]
        p1 = prev[7:8, :]
        p2 = prev[6:7, :]
        s1 = jnp.where(row == 0, p1, pltpu.roll(a, 1, 0))
        s2 = jnp.where(row == 0, p2, jnp.where(row == 1, p1, pltpu.roll(a, 2, 0)))
        h = _conv_gate(a, gt, s1, s2, cw_ref[j])
        carry_ref[j] = a[tm - V7X_SUBLANES:, :]
        acc_ref[...] += jnp.dot(h.astype(BF16), wout_ref[j], preferred_element_type=F32)
        return carry

    lax.fori_loop(0, FFN_NCHUNK, body, 0)
    o_ref[...] = _deepnorm_ln(x_ref[...], acc_ref[...], g_ref[...], b_ref[...])

    @pl.when(i == last)
    def _():
        cs_ref[...] = carry_ref[:, V7X_SUBLANES - (CONV_WIDTH - 1):, :]


def _ffn(x, wa, wg, cw, wout, g, bta, *, tm):
    b, t, _ = x.shape
    grid = (b, t // tm)
    tok = pl.BlockSpec((None, tm, D_MODEL), lambda bi, i: (bi, i, 0))
    cs_shape = (FFN_NCHUNK, CONV_WIDTH - 1, FFN_CHUNK)
    return pl.pallas_call(
        functools.partial(_ffn_kernel, tm=tm),
        grid=grid,
        in_specs=[tok, _full(wa.shape), _full(wg.shape), _full(cw.shape), _full(wout.shape),
                  _full(g.shape), _full(bta.shape)],
        out_specs=[tok, pl.BlockSpec((None,) + cs_shape, lambda bi, i: (bi, 0, 0, 0))],
        out_shape=[
            jax.ShapeDtypeStruct((b, t, D_MODEL), F32),
            jax.ShapeDtypeStruct((b,) + cs_shape, F32),
        ],
        scratch_shapes=[
            pltpu.VMEM((tm, D_MODEL), F32),
            pltpu.VMEM((FFN_NCHUNK, V7X_SUBLANES, FFN_CHUNK), F32),
        ],
        compiler_params=_params("arbitrary", "arbitrary"),
        name="conv_ffn",
    )(x, wa, wg, cw, wout, g, bta)


def _sample_attn_kernel(sink_ref, x_ref, wq_ref, wkv_ref, wo_ref, cos_ref, sin_ref, ck_ref, cv_ref,
                        g_ref, b_ref, o_ref, kw_ref, vw_ref, q_scr, kv_scr, o_scr, *, bb):
    i = pl.program_id(0)
    last = pl.num_programs(0) - 1
    n = x_ref.shape[0]
    w = WINDOW

    @pl.when(i == 0)
    def _():
        xb = x_ref[...].astype(BF16)
        lane = lax.broadcasted_iota(jnp.int32, (n, V7X_LANES), 1)
        first_half = (lane & 63) < 32
        cos = cos_ref[...]
        sin = sin_ref[...]
        zq = jnp.dot(xb, wq_ref[...], preferred_element_type=F32)
        for c in range(NQ // 128):
            q_scr[:, c * 128:(c + 1) * 128] = _rope_128(
                zq[:, c * 128:(c + 1) * 128], cos, sin, first_half) * ATTN_SCALE
        zkv = jnp.dot(xb, wkv_ref[...], preferred_element_type=F32)
        for c in range(NKV // 128):
            kv_scr[:, c * 128:(c + 1) * 128] = _rope_128(
                zkv[:, c * 128:(c + 1) * 128], cos, sin, first_half)
        kv_scr[:, NKV:] = zkv[:, NKV:]

    r16 = lax.broadcasted_iota(jnp.int32, (ATTN_HEADS, NKV), 0)
    l16 = lax.broadcasted_iota(jnp.int32, (ATTN_HEADS, NKV), 1)
    diag = (l16 >> 6) == (r16 & (ATTN_KV_HEADS - 1))
    rowkey = lax.broadcasted_iota(jnp.int32, (w, NKV), 0)
    sink = sink_ref[...]

    def token(bi, carry):
        t = i * bb + bi
        qrow = q_scr[pl.ds(t, 1), :]
        knew = kv_scr[pl.ds(t, 1), 0:NKV]
        vnew = kv_scr[pl.ds(t, 1), NKV:]
        kc = ck_ref[bi]
        vc = cv_ref[bi]
        kw_ref[bi] = jnp.where(rowkey == w - 1, knew, pltpu.roll(kc, w - 1, 0))
        vw_ref[bi] = jnp.where(rowkey == w - 1, vnew, pltpu.roll(vc, w - 1, 0))
        qrep = jnp.broadcast_to(qrow[:, 3 * NKV:], (ATTN_HEADS, NKV))
        for hh in (2, 1, 0):
            qrep = jnp.where(r16 < (hh + 1) * ATTN_KV_HEADS,
                             jnp.broadcast_to(qrow[:, hh * NKV:(hh + 1) * NKV], (ATTN_HEADS, NKV)), qrep)
        qbd = jnp.where(diag, qrep, 0.0)
        s = lax.dot_general(qbd.astype(BF16), kc.astype(BF16), (((1,), (1,)), ((), ())),
                            preferred_element_type=F32)
        s_new = jnp.sum(qbd * knew, axis=-1, keepdims=True)
        m = jnp.maximum(jnp.maximum(jnp.max(s, axis=-1, keepdims=True), s_new), sink)
        e = jnp.exp(s - m)
        e_new = jnp.exp(s_new - m)
        denom = jnp.sum(e, axis=-1, keepdims=True) + e_new + jnp.exp(sink - m)
        o = jnp.dot(e.astype(BF16), vc.astype(BF16), preferred_element_type=F32) + e_new * vnew
        o = jnp.where(diag, o / denom, 0.0)
        for hh in range(ATTN_GROUP):
            part = jnp.where((r16 >> 2) == hh, o, 0.0)
            o_scr[pl.ds(t, 1), hh * NKV:(hh + 1) * NKV] = jnp.sum(part, axis=0, keepdims=True)
        return carry

    lax.fori_loop(0, bb, token, 0)

    @pl.when(i == last)
    def _():
        out = jnp.dot(o_scr[...].astype(BF16), wo_ref[...], preferred_element_type=F32)
        o_ref[...] = _deepnorm_ln(x_ref[...], out, g_ref[...], b_ref[...])


def _sample_attn(sinks_perm, x, wq_perm, wkv, wo_perm, cos, sin, cache_k, cache_v, g, bta, *, bb):
    n = x.shape[0]
    grid = (n // bb,)
    cache = pl.BlockSpec((bb, WINDOW, NKV), lambda i: (i, 0, 0))
    return pl.pallas_call(
        functools.partial(_sample_attn_kernel, bb=bb),
        grid=grid,
        in_specs=[
            _full(sinks_perm.shape),
            _full(x.shape), _full(wq_perm.shape), _full(wkv.shape), _full(wo_perm.shape),
            _full(cos.shape), _full(sin.shape), cache, cache, _full(g.shape), _full(bta.shape),
        ],
        out_specs=[_full((n, D_MODEL)), cache, cache],
        out_shape=[
            jax.ShapeDtypeStruct((n, D_MODEL), F32),
            jax.ShapeDtypeStruct(cache_k.shape, F32),
            jax.ShapeDtypeStruct(cache_v.shape, F32),
        ],
        scratch_shapes=[
            pltpu.VMEM((n, NQ), F32),
            pltpu.VMEM((n, 2 * NKV), F32),
            pltpu.VMEM((n, NQ), F32),
        ],
        compiler_params=_params("arbitrary"),
        name="sample_attn",
    )(sinks_perm, x, wq_perm, wkv, wo_perm, cos, sin, cache_k, cache_v, g, bta)


def _sample_ret_in_kernel(x_ref, w_ref, cos_ref, sin_ref, qt_ref, kt_ref, v_ref, sg_ref):
    n = x_ref.shape[0]
    xb = x_ref[...].astype(BF16)
    lane = lax.broadcasted_iota(jnp.int32, (n, V7X_LANES), 1)
    even = (lane & 1) == 0
    for sec, out_ref, scale in ((0, qt_ref, None), (1, kt_ref, RET_K_SCALE)):
        z = jnp.dot(xb, w_ref[:, sec * RET_NQK:(sec + 1) * RET_NQK], preferred_element_type=F32)
        for c in range(RET_NQK // 128):
            t0 = (c % 2) * 128
            zc = _pair_rot_128(z[:, c * 128:(c + 1) * 128], cos_ref[:, t0:t0 + 128],
                               sin_ref[:, t0:t0 + 128], even)
            if scale is not None:
                zc = zc * scale
            out_ref[c * 128:(c + 1) * 128, :] = zc.T
    base = 2 * RET_NQK
    v_ref[...] = jnp.dot(xb, w_ref[:, base:base + RET_NV], preferred_element_type=F32)
    base = 2 * RET_NQK + RET_NV
    sg_ref[...] = _silu(jnp.dot(xb, w_ref[:, base:base + RET_NV], preferred_element_type=F32))


def _sample_ret_in(x, w_in_bf16, cos, sin):
    n = x.shape[0]
    return pl.pallas_call(
        _sample_ret_in_kernel,
        grid=(1,),
        in_specs=[_full(x.shape), _full(w_in_bf16.shape), _full(cos.shape), _full(sin.shape)],
        out_specs=[_full((RET_NQK, n)), _full((RET_NQK, n)), _full((n, RET_NV)), _full((n, RET_NV))],
        out_shape=[
            jax.ShapeDtypeStruct((RET_NQK, n), F32),
            jax.ShapeDtypeStruct((RET_NQK, n), F32),
            jax.ShapeDtypeStruct((n, RET_NV), F32),
            jax.ShapeDtypeStruct((n, RET_NV), F32),
        ],
        compiler_params=_params("arbitrary"),
        name="sample_ret_in",
    )(x, w_in_bf16, cos, sin)


def _sample_ret_state_kernel(gam_ref, qt_ref, kt_ref, v_ref, s_ref, o_ref, sn_ref, *, bb):
    i = pl.program_id(0)
    n = v_ref.shape[0]
    lane = lax.broadcasted_iota(jnp.int32, (1, n), 1)
    for bi in range(bb):
        t = i * bb + bi
        onehot = (lane == t).astype(F32)
        for h in range(RET_HEADS):
            krows = slice(h * RET_KEY_DIM, (h + 1) * RET_KEY_DIM)
            vcols = slice(h * RET_VALUE_DIM, (h + 1) * RET_VALUE_DIM)
            qc = jnp.sum(qt_ref[krows, :] * onehot, axis=1, keepdims=True)
            kc = jnp.sum(kt_ref[krows, :] * onehot, axis=1, keepdims=True)
            vrow = v_ref[pl.ds(t, 1), vcols]
            s = s_ref[bi, h]
            gamma = gam_ref[h]
            cross = jnp.sum(s * qc, axis=0, keepdims=True)
            qk = jnp.sum(qc * kc, axis=0, keepdims=True)
            o_ref[pl.ds(t, 1), vcols] = qk * vrow + cross * gamma
            sn_ref[bi, h] = s * gamma + kc * vrow


def _sample_ret_state(gamma, qt, kt, v, state, *, bb):
    n = v.shape[0]
    st = pl.BlockSpec((bb, RET_HEADS, RET_KEY_DIM, RET_VALUE_DIM), lambda i: (i, 0, 0, 0))
    return pl.pallas_call(
        functools.partial(_sample_ret_state_kernel, bb=bb),
        grid=(n // bb,),
        in_specs=[pl.BlockSpec(memory_space=pltpu.SMEM), _full(qt.shape), _full(kt.shape),
                  _full(v.shape), st],
        out_specs=[_full((n, RET_NV)), st],
        out_shape=[
            jax.ShapeDtypeStruct((n, RET_NV), F32),
            jax.ShapeDtypeStruct(state.shape, F32),
        ],
        compiler_params=_params("arbitrary"),
        name="sample_ret_state",
    )(gamma, qt, kt, v, state)


def _sample_ret_out_kernel(o_ref, sg_ref, x_ref, wo_ref, g_ref, b_ref, y_ref):
    n = x_ref.shape[0]
    acc = jnp.zeros((n, D_MODEL), F32)
    for h in range(RET_HEADS):
        vcols = slice(h * RET_VALUE_DIM, (h + 1) * RET_VALUE_DIM)
        o = o_ref[:, vcols]
        mu = jnp.mean(o, axis=-1, keepdims=True)
        d = o - mu
        var = jnp.mean(d * d, axis=-1, keepdims=True)
        gated = (sg_ref[:, vcols] * (d * lax.rsqrt(var + GN_EPS))).astype(BF16)
        acc = acc + jnp.dot(gated, wo_ref[vcols, :], preferred_element_type=F32)
    y_ref[...] = _deepnorm_ln(x_ref[...], acc, g_ref[...], b_ref[...])


def _sample_ret_out(o, sg, x, wo_bf16, g, bta):
    args = (o, sg, x, wo_bf16, g, bta)
    return pl.pallas_call(
        _sample_ret_out_kernel,
        grid=(1,),
        in_specs=[_full(a.shape) for a in args],
        out_specs=_full(x.shape),
        out_shape=jax.ShapeDtypeStruct(x.shape, F32),
        compiler_params=_params("arbitrary"),
        name="sample_ret_out",
    )(*args)


def _sample_ffn_kernel(x_ref, b0_ref, b1_ref, wa_ref, wg_ref, cw_ref, wout_ref, g_ref, b_ref,
                       y_ref, a_ref):
    n = x_ref.shape[0]
    xb = x_ref[...].astype(BF16)
    acc = jnp.zeros((n, D_MODEL), F32)
    for j in range(FFN_NCHUNK):
        cols = slice(j * FFN_CHUNK, (j + 1) * FFN_CHUNK)
        a = jnp.dot(xb, wa_ref[j], preferred_element_type=F32)
        gt = jnp.dot(xb, wg_ref[j], preferred_element_type=F32)
        h = _conv_gate(a, gt, b1_ref[:, cols], b0_ref[:, cols], cw_ref[j])
        a_ref[:, cols] = a
        acc = acc + jnp.dot(h.astype(BF16), wout_ref[j], preferred_element_type=F32)
    y_ref[...] = _deepnorm_ln(x_ref[...], acc, g_ref[...], b_ref[...])


def _sample_ffn(x, buf0, buf1, wa, wg, cw, wout, g, bta):
    args = (x, buf0, buf1, wa, wg, cw, wout, g, bta)
    return pl.pallas_call(
        _sample_ffn_kernel,
        grid=(1,),
        in_specs=[_full(a.shape) for a in args],
        out_specs=[_full(x.shape), _full(buf0.shape)],
        out_shape=[jax.ShapeDtypeStruct(x.shape, F32), jax.ShapeDtypeStruct(buf0.shape, F32)],
        compiler_params=_params("arbitrary"),
        name="sample_ffn",
    )(*args)


def _head_major_to_group_minor(w, axis):
    shape = w.shape
    split = shape[:axis] + (ATTN_KV_HEADS, ATTN_GROUP, ATTN_HEAD_DIM) + shape[axis + 1:]
    return jnp.swapaxes(w.reshape(split), axis, axis + 1).reshape(shape)


def kernel(x_prompt, x_sample, cache_k_win, cache_v_win, state_ret, state_conv, attn_w_qkv, attn_sinks, attn_w_o, ret_w_in, ret_w_o, ffn_w_in, ffn_conv_w, ffn_conv_b, ffn_w_out, ln_mix_g, ln_mix_b, ln_ffn_g, ln_ffn_b):
    bp, tp, _ = x_prompt.shape
    ns = x_sample.shape[0]
    assert x_sample.shape[1] == 1, "the sample group carries one new token per sequence"
    xp = x_prompt
    xs = x_sample.reshape(ns, D_MODEL)
    pos_p = jnp.arange(tp)
    pos_s = jnp.full((ns,), PAST_LEN, jnp.int32)
    rope_p = _rope_tables(pos_p)
    rope_s = _rope_tables(pos_s)
    rot_p = _ret_rot_tables(pos_p)
    rot_s = _ret_rot_tables(pos_s)
    decay_in, q_decay, k_decay, chunk_decay = _ret_decay_tables(RET_CHUNK)
    gamma = jnp.exp(_ret_log_gamma())
    row = lambda v: v.reshape(1, D_MODEL)

    tm = min(512, tp)
    kp_l, vp_l, ks_l, vs_l, rp_l, rs_l, cp_l, cs_l = [], [], [], [], [], [], [], []
    for i in range(DEPTH):
        j = i // N_MIXERS
        g_mix, b_mix = row(ln_mix_g[i]), row(ln_mix_b[i])
        if i % N_MIXERS == 0:
            w_qkv = attn_w_qkv[j].astype(BF16)
            w_o = attn_w_o[j].astype(BF16)
            q, k2, v2, kf, vf = _attn_qkv(xp, w_qkv, *rope_p, tm=tm)
            xp = _attn_core(attn_sinks[j], q, k2, v2, xp, w_o, g_mix, b_mix, tq=tm)
            kp_l.append(kf.reshape(bp, WINDOW, ATTN_KV_HEADS, ATTN_HEAD_DIM))
            vp_l.append(vf.reshape(bp, WINDOW, ATTN_KV_HEADS, ATTN_HEAD_DIM))
            xs, kw, vw = _sample_attn(
                attn_sinks[j].reshape(ATTN_KV_HEADS, ATTN_GROUP).T.reshape(ATTN_HEADS, 1), xs,
                _head_major_to_group_minor(w_qkv[:, :NQ], 1), w_qkv[:, NQ:],
                _head_major_to_group_minor(w_o, 0), *rope_s,
                cache_k_win[j].reshape(ns, WINDOW, NKV), cache_v_win[j].reshape(ns, WINDOW, NKV),
                g_mix, b_mix, bb=min(16, ns))
            ks_l.append(kw.reshape(ns, WINDOW, ATTN_KV_HEADS, ATTN_HEAD_DIM))
            vs_l.append(vw.reshape(ns, WINDOW, ATTN_KV_HEADS, ATTN_HEAD_DIM))
        else:
            w_in = ret_w_in[j].astype(BF16)
            w_o = ret_w_o[j].astype(BF16)
            q, k, v, sg = _ret_in(xp, w_in, *rot_p, tm=tm)
            xp, rp = _ret_core(chunk_decay, q, k, v, sg, xp, decay_in, q_decay, k_decay, w_o,
                               g_mix, b_mix, ct=tm)
            rp_l.append(rp)
            qt, kt, vs, sgs = _sample_ret_in(xs, w_in, *rot_s)
            os_, rs = _sample_ret_state(gamma, qt, kt, vs, state_ret[j], bb=2)
            xs = _sample_ret_out(os_, sgs, xs, w_o, g_mix, b_mix)
            rs_l.append(rs)
        w_in = ffn_w_in[i]
        chunked = lambda w: w.reshape(D_MODEL, FFN_NCHUNK, FFN_CHUNK).swapaxes(0, 1).astype(BF16)
        wa, wg = chunked(w_in[:, :D_FF]), chunked(w_in[:, D_FF:])
        wout = ffn_w_out[i].reshape(FFN_NCHUNK, FFN_CHUNK, D_MODEL).astype(BF16)
        cw = jnp.concatenate(
            [ffn_conv_w[i], ffn_conv_b[i][None], jnp.zeros((V7X_SUBLANES - CONV_WIDTH - 1, D_FF), F32)], axis=0)
        cw = cw.reshape(V7X_SUBLANES, FFN_NCHUNK, FFN_CHUNK).swapaxes(0, 1)
        g_ffn, b_ffn = row(ln_ffn_g[i]), row(ln_ffn_b[i])
        xp, cp = _ffn(xp, wa, wg, cw, wout, g_ffn, b_ffn, tm=tm)
        cp_l.append(cp.swapaxes(1, 2).reshape(bp, CONV_WIDTH - 1, D_FF))
        buf = state_conv[i]
        xs, a_new = _sample_ffn(xs, buf[:, 0, :], buf[:, 1, :], wa, wg, cw, wout, g_ffn, b_ffn)
        cs_l.append(jnp.stack([buf[:, 1, :], a_new], axis=1))
    return (xp, xs.reshape(ns, 1, D_MODEL),
            jnp.stack(kp_l), jnp.stack(vp_l), jnp.stack(rp_l), jnp.stack(cp_l),
            jnp.stack(ks_l), jnp.stack(vs_l), jnp.stack(rs_l), jnp.stack(cs_l))
```

```python
import functools

import jax
import jax.numpy as jnp
from jax import lax
from jax.experimental import pallas as pl
from jax.experimental.pallas import tpu as pltpu

F32 = jnp.float32
BF16 = jnp.bfloat16

D_MODEL = 1024
DEPTH = 4
PAST_LEN = 8192
N_MIXERS = 2
ATTN_HEAD_DIM = 64
ATTN_HEADS = 16
ATTN_KV_HEADS = 4
ATTN_GROUP = ATTN_HEADS // ATTN_KV_HEADS
WINDOW = 128
ROPE_THETA = 10000.0
RET_KEY_DIM = 256
RET_HEADS = 4
RET_VALUE_DIM = 512
RET_CHUNK = 128
D_FF = 2816
CONV_WIDTH = 3
LN_EPS = 1e-5
GN_EPS = 1e-5
DEEPNORM_ALPHA = (2.0 * DEPTH) ** 0.25
ATTN_SCALE = ATTN_HEAD_DIM ** -0.5
RET_K_SCALE = RET_KEY_DIM ** -0.5

V7X_LANES = 128
V7X_SUBLANES = 8
V7X_VMEM_LIMIT_BYTES = 56 * 1024 * 1024

NQ = ATTN_HEADS * ATTN_HEAD_DIM
NKV = ATTN_KV_HEADS * ATTN_HEAD_DIM
RET_NQK = RET_HEADS * RET_KEY_DIM
RET_NV = RET_HEADS * RET_VALUE_DIM
FFN_CHUNK = 256
FFN_NCHUNK = D_FF // FFN_CHUNK
MASKED_SCORE = -1e30


def _params(*semantics, flags=None):
    return pltpu.CompilerParams(
        dimension_semantics=semantics, vmem_limit_bytes=V7X_VMEM_LIMIT_BYTES, flags=flags
    )


def _full(shape):
    n = len(shape)
    return pl.BlockSpec(shape, lambda *_: (0,) * n)


def _deepnorm_ln(x, m, g, b):
    y = DEEPNORM_ALPHA * x + m
    mu = jnp.mean(y, axis=-1, keepdims=True)
    d = y - mu
    var = jnp.mean(d * d, axis=-1, keepdims=True)
    return d * lax.rsqrt(var + LN_EPS) * g + b


def _silu(z):
    return z * jax.nn.sigmoid(z)


def _rope_tables(pos):
    half = ATTN_HEAD_DIM // 2
    inv_freq = ROPE_THETA ** (-jnp.arange(half, dtype=F32) / half)
    ang = pos.astype(F32)[:, None] * inv_freq[None, :]
    cos, sin = jnp.cos(ang), jnp.sin(ang)
    return jnp.tile(cos, (1, 4)), jnp.concatenate([-sin, sin, -sin, sin], axis=1)


def _ret_rot_tables(pos):
    d = RET_KEY_DIM
    angle = 1.0 / (10000.0 ** jnp.linspace(0.0, 1.0, d // 2, dtype=F32))
    ang = pos.astype(F32)[:, None] * angle[None, :]
    cos, sin = jnp.cos(ang), jnp.sin(ang)
    cos_t = jnp.repeat(cos, 2, axis=1)
    sin_t = jnp.stack([-sin, sin], axis=-1).reshape(pos.shape[0], d)
    return cos_t, sin_t


def _ret_log_gamma():
    return jnp.log(1.0 - 2.0 ** (-5.0 - jnp.arange(RET_HEADS, dtype=F32)))


def _ret_decay_tables(chunk):
    log_gamma = _ret_log_gamma()
    idx = jnp.arange(chunk, dtype=F32)
    rel = idx[:, None] - idx[None, :]
    decay_in = jnp.where(
        rel[None] >= 0,
        jnp.exp(log_gamma[:, None, None] * jnp.maximum(rel, 0.0)[None]),
        0.0,
    )
    q_decay = jnp.exp(log_gamma[:, None] * (idx[None, :] + 1.0))
    k_decay = jnp.exp(log_gamma[:, None] * (chunk - 1.0 - idx[None, :]))
    chunk_decay = jnp.exp(log_gamma * chunk)
    lane_bcast = lambda t: jnp.broadcast_to(t[:, :, None], (RET_HEADS, chunk, V7X_LANES))
    return decay_in, lane_bcast(q_decay), lane_bcast(k_decay), chunk_decay


def _rope_128(z, cos, sin_signed, first_half):
    partner = jnp.where(first_half, pltpu.roll(z, 96, 1), pltpu.roll(z, 32, 1))
    return z * cos + partner * sin_signed


def _pair_rot_128(z, cos, sin_signed, even_lane):
    partner = jnp.where(even_lane, pltpu.roll(z, 127, 1), pltpu.roll(z, 1, 1))
    return z * cos + partner * sin_signed


def _attn_qkv_kernel(x_ref, w_ref, cos_ref, sin_ref, q_ref, k2_ref, v2_ref, kf_ref, vf_ref, *, tm):
    i = pl.program_id(1)
    last = pl.num_programs(1) - 1
    xb = x_ref[...].astype(BF16)
    cos = cos_ref[...]
    sin = sin_ref[...]
    lane = lax.broadcasted_iota(jnp.int32, (tm, V7X_LANES), 1)
    first_half = (lane & 63) < 32
    lo = lane < 64

    def dup(z, take_low):
        zr = pltpu.roll(z, 64, 1)
        return jnp.where(lo, z, zr) if take_low else jnp.where(lo, zr, z)

    for j in range(NQ // 512):
        z = jnp.dot(xb, w_ref[:, j * 512:(j + 1) * 512], preferred_element_type=F32)
        for c in range(4):
            zc = _rope_128(z[:, c * 128:(c + 1) * 128], cos, sin, first_half) * ATTN_SCALE
            q_ref[:, j * 512 + c * 128:j * 512 + (c + 1) * 128] = zc.astype(BF16)
    zkv = jnp.dot(xb, w_ref[:, NQ:NQ + 2 * NKV], preferred_element_type=F32)
    for c in range(NKV // 128):
        kc = _rope_128(zkv[:, c * 128:(c + 1) * 128], cos, sin, first_half)
        vc = zkv[:, NKV + c * 128:NKV + (c + 1) * 128]
        for gg in range(2):
            g = 2 * c + gg
            k2_ref[:, g * 128:(g + 1) * 128] = dup(kc, gg == 0).astype(BF16)
            v2_ref[:, g * 128:(g + 1) * 128] = dup(vc, gg == 0).astype(BF16)

        @pl.when(i == last)
        def _():
            kf_ref[:, c * 128:(c + 1) * 128] = kc[tm - WINDOW:, :]
            vf_ref[:, c * 128:(c + 1) * 128] = vc[tm - WINDOW:, :]


def _attn_qkv(x, w_qkv_bf16, cos, sin, *, tm):
    b, t, _ = x.shape
    grid = (b, t // tm)
    tok = lambda w: pl.BlockSpec((None, tm, w), lambda bi, i: (bi, i, 0))
    tab = pl.BlockSpec((tm, V7X_LANES), lambda bi, i: (i, 0))
    tail = pl.BlockSpec((None, WINDOW, NKV), lambda bi, i: (bi, 0, 0))
    return pl.pallas_call(
        functools.partial(_attn_qkv_kernel, tm=tm),
        grid=grid,
        in_specs=[tok(D_MODEL), _full(w_qkv_bf16.shape), tab, tab],
        out_specs=[tok(NQ), tok(2 * NKV), tok(2 * NKV), tail, tail],
        out_shape=[
            jax.ShapeDtypeStruct((b, t, NQ), BF16),
            jax.ShapeDtypeStruct((b, t, 2 * NKV), BF16),
            jax.ShapeDtypeStruct((b, t, 2 * NKV), BF16),
            jax.ShapeDtypeStruct((b, WINDOW, NKV), F32),
            jax.ShapeDtypeStruct((b, WINDOW, NKV), F32),
        ],
        compiler_params=_params("arbitrary", "arbitrary"),
        name="attn_qkv",
    )(x, w_qkv_bf16, cos, sin)


def _attn_core_kernel(sink_ref, q_ref, kc_ref, kp_ref, vc_ref, vp_ref, x_ref, wo_ref, g_ref, b_ref,
                      o_ref, kcat, vcat, oscr, *, tq):
    i = pl.program_id(1)
    kcat[0:WINDOW, :] = kp_ref[...]
    kcat[WINDOW:, :] = kc_ref[...]
    vcat[0:WINDOW, :] = vp_ref[...]
    vcat[WINDOW:, :] = vc_ref[...]
    w = WINDOW
    ii = lax.broadcasted_iota(jnp.int32, (2 * w, 2 * w), 0) & (w - 1)
    jj = lax.broadcasted_iota(jnp.int32, (2 * w, 2 * w), 1)
    band = (jj >= ii) & (jj <= ii + w)
    first_key = jnp.where(i == 0, w, 0)
    lo = lax.broadcasted_iota(jnp.int32, (w, V7X_LANES), 1) < 64
    top = lax.broadcasted_iota(jnp.int32, (2 * w, 1), 0) < w
    zero = jnp.zeros((w, V7X_LANES), BF16)
    for qb in range(tq // w):
        mask = band & (jj >= first_key) if qb == 0 else band
        rows = slice(qb * w, (qb + 1) * w)
        keys = slice(qb * w, (qb + 2) * w)
        for p in range(ATTN_HEADS // 2):
            g = p // (ATTN_GROUP // 2)
            cols = slice(p * 128, (p + 1) * 128)
            gcols = slice(g * 128, (g + 1) * 128)
            q2 = q_ref[rows, cols]
            qs = jnp.concatenate([jnp.where(lo, q2, zero), jnp.where(lo, zero, q2)], axis=0)
            s = lax.dot_general(qs, kcat[keys, gcols], (((1,), (1,)), ((), ())),
                                preferred_element_type=F32)
            s = jnp.where(mask, s, MASKED_SCORE)
            sink = jnp.where(top, sink_ref[2 * p], sink_ref[2 * p + 1])
            m = jnp.maximum(jnp.max(s, axis=-1, keepdims=True), sink)
            e = jnp.exp(s - m)
            denom = jnp.sum(e, axis=-1, keepdims=True) + jnp.exp(sink - m)
            o2 = jnp.dot(e.astype(BF16), vcat[keys, gcols], preferred_element_type=F32)
            o2 = o2 / denom
            oscr[rows, cols] = jnp.where(lo, o2[:w], o2[w:]).astype(BF16)
    out = jnp.dot(oscr[...], wo_ref[...], preferred_element_type=F32)
    o_ref[...] = _deepnorm_ln(x_ref[...], out, g_ref[...], b_ref[...])


def _attn_core(sinks, q, k2, v2, x, wo_bf16, g, bta, *, tq):
    b, t, _ = x.shape
    grid = (b, t // tq)
    r = tq // WINDOW
    cur = lambda w: pl.BlockSpec((None, tq, w), lambda bi, i: (bi, i, 0))
    prev = pl.BlockSpec((None, WINDOW, 2 * NKV), lambda bi, i: (bi, jnp.maximum(i * r - 1, 0), 0))
    return pl.pallas_call(
        functools.partial(_attn_core_kernel, tq=tq),
        grid=grid,
        in_specs=[
            pl.BlockSpec(memory_space=pltpu.SMEM),
            cur(NQ), cur(2 * NKV), prev, cur(2 * NKV), prev, cur(D_MODEL),
            _full(wo_bf16.shape), _full(g.shape), _full(bta.shape),
        ],
        out_specs=cur(D_MODEL),
        out_shape=jax.ShapeDtypeStruct((b, t, D_MODEL), F32),
        scratch_shapes=[
            pltpu.VMEM((tq + WINDOW, 2 * NKV), BF16),
            pltpu.VMEM((tq + WINDOW, 2 * NKV), BF16),
            pltpu.VMEM((tq, NQ), BF16),
        ],
        compiler_params=_params("arbitrary", "arbitrary"),
        name="attn_core",
    )(sinks, q, k2, k2, v2, v2, x, wo_bf16, g, bta)


def _ret_in_kernel(x_ref, w_ref, cos_ref, sin_ref, q_ref, k_ref, v_ref, sg_ref, *, tm):
    xb = x_ref[...].astype(BF16)
    lane = lax.broadcasted_iota(jnp.int32, (tm, V7X_LANES), 1)
    even = (lane & 1) == 0
    nc = 512
    for sec, out_ref, scale in ((0, q_ref, None), (1, k_ref, RET_K_SCALE)):
        for j in range(RET_NQK // nc):
            z = jnp.dot(xb, w_ref[:, sec * RET_NQK + j * nc:sec * RET_NQK + (j + 1) * nc],
                        preferred_element_type=F32)
            for c in range(nc // 128):
                t0 = (c % 2) * 128
                zc = _pair_rot_128(z[:, c * 128:(c + 1) * 128], cos_ref[:, t0:t0 + 128],
                                   sin_ref[:, t0:t0 + 128], even)
                if scale is not None:
                    zc = zc * scale
                out_ref[:, j * nc + c * 128:j * nc + (c + 1) * 128] = zc.astype(BF16)
    base = 2 * RET_NQK
    for j in range(RET_NV // nc):
        z = jnp.dot(xb, w_ref[:, base + j * nc:base + (j + 1) * nc], preferred_element_type=F32)
        v_ref[:, j * nc:(j + 1) * nc] = z.astype(BF16)
    base = 2 * RET_NQK + RET_NV
    for j in range(RET_NV // nc):
        z = jnp.dot(xb, w_ref[:, base + j * nc:base + (j + 1) * nc], preferred_element_type=F32)
        sg_ref[:, j * nc:(j + 1) * nc] = _silu(z).astype(BF16)


def _ret_in(x, w_in_bf16, cos, sin, *, tm):
    b, t, _ = x.shape
    grid = (b, t // tm)
    tok = lambda w: pl.BlockSpec((None, tm, w), lambda bi, i: (bi, i, 0))
    tab = pl.BlockSpec((tm, RET_KEY_DIM), lambda bi, i: (i, 0))
    return pl.pallas_call(
        functools.partial(_ret_in_kernel, tm=tm),
        grid=grid,
        in_specs=[tok(D_MODEL), _full(w_in_bf16.shape), tab, tab],
        out_specs=[tok(RET_NQK), tok(RET_NQK), tok(RET_NV), tok(RET_NV)],
        out_shape=[
            jax.ShapeDtypeStruct((b, t, RET_NQK), BF16),
            jax.ShapeDtypeStruct((b, t, RET_NQK), BF16),
            jax.ShapeDtypeStruct((b, t, RET_NV), BF16),
            jax.ShapeDtypeStruct((b, t, RET_NV), BF16),
        ],
        compiler_params=_params("arbitrary", "arbitrary"),
        name="ret_in",
    )(x, w_in_bf16, cos, sin)


def _ret_core_kernel(cd_ref, q_ref, k_ref, v_ref, sg_ref, x_ref, din_ref, qd_ref, kd_ref, wo_ref,
                     g_ref, b_ref, o_ref, st_ref, s_scr, gated_scr, *, ct):
    i = pl.program_id(1)
    last = pl.num_programs(1) - 1
    c = RET_CHUNK

    @pl.when(i == 0)
    def _():
        s_scr[...] = jnp.zeros_like(s_scr)

    heads = range(RET_HEADS)
    kcols = [slice(h * RET_KEY_DIM, (h + 1) * RET_KEY_DIM) for h in heads]
    vcols = [slice(h * RET_VALUE_DIM, (h + 1) * RET_VALUE_DIM) for h in heads]
    nt = (((1,), (1,)), ((), ()))
    tn = (((0,), (0,)), ((), ()))
    for ci in range(ct // c):
        rows = slice(ci * c, (ci + 1) * c)
        qs = [q_ref[rows, kcols[h]] for h in heads]
        ks = [k_ref[rows, kcols[h]] for h in heads]
        vs = [v_ref[rows, vcols[h]] for h in heads]
        inner = [lax.dot_general(qs[h], ks[h], nt, preferred_element_type=F32) for h in heads]
        inner = [(inner[h] * din_ref[h]).astype(BF16) for h in heads]
        o = []
        for h in heads:
            qd = jnp.concatenate([qd_ref[h]] * (RET_KEY_DIM // V7X_LANES), axis=1)
            qdec = (qs[h].astype(F32) * qd).astype(BF16)
            o.append(jnp.dot(inner[h], vs[h], preferred_element_type=F32)
                     + jnp.dot(qdec, s_scr[h].astype(BF16), preferred_element_type=F32))
        for h in heads:
            kd = jnp.concatenate([kd_ref[h]] * (RET_KEY_DIM // V7X_LANES), axis=1)
            kdec = (ks[h].astype(F32) * kd).astype(BF16)
            s_scr[h] = s_scr[h] * cd_ref[h] + lax.dot_general(
                kdec, vs[h], tn, preferred_element_type=F32)
        for h in heads:
            mu = jnp.mean(o[h], axis=-1, keepdims=True)
            d = o[h] - mu
            var = jnp.mean(d * d, axis=-1, keepdims=True)
            on = d * lax.rsqrt(var + GN_EPS)
            gated_scr[rows, vcols[h]] = (sg_ref[rows, vcols[h]].astype(F32) * on).astype(BF16)
        out = jnp.dot(gated_scr[rows, :], wo_ref[...], preferred_element_type=F32)
        o_ref[rows, :] = _deepnorm_ln(x_ref[rows, :], out, g_ref[...], b_ref[...])

    @pl.when(i == last)
    def _():
        st_ref[...] = s_scr[...]


def _ret_core(chunk_decay, q, k, v, sg, x, decay_in, q_decay, k_decay, wo_bf16, g, bta, *, ct):
    b, t, _ = x.shape
    grid = (b, t // ct)
    tok = lambda w: pl.BlockSpec((None, ct, w), lambda bi, i: (bi, i, 0))
    st_shape = (RET_HEADS, RET_KEY_DIM, RET_VALUE_DIM)
    return pl.pallas_call(
        functools.partial(_ret_core_kernel, ct=ct),
        grid=grid,
        in_specs=[
            pl.BlockSpec(memory_space=pltpu.SMEM),
            tok(RET_NQK), tok(RET_NQK), tok(RET_NV), tok(RET_NV), tok(D_MODEL),
            _full(decay_in.shape), _full(q_decay.shape), _full(k_decay.shape),
            _full(wo_bf16.shape), _full(g.shape), _full(bta.shape),
        ],
        out_specs=[tok(D_MODEL), pl.BlockSpec((None,) + st_shape, lambda bi, i: (bi, 0, 0, 0))],
        out_shape=[
            jax.ShapeDtypeStruct((b, t, D_MODEL), F32),
            jax.ShapeDtypeStruct((b,) + st_shape, F32),
        ],
        scratch_shapes=[pltpu.VMEM(st_shape, F32), pltpu.VMEM((ct, RET_NV), BF16)],
        compiler_params=_params("arbitrary", "arbitrary"),
        name="ret_core",
    )(chunk_decay, q, k, v, sg, x, decay_in, q_decay, k_decay, wo_bf16, g, bta)


def _conv_gate(a, gt, s1, s2, cw):
    c = cw[3:4, :] + s2 * cw[0:1, :]
    c = c + s1 * cw[1:2, :]
    c = c + a * cw[2:3, :]
    return _silu(c) * gt


def _shift_rows(a, p1, p2, row8):
    r1 = pltpu.roll(a, 1, 0)
    r2 = pltpu.roll(a, 2, 0)
    s = V7X_SUBLANES
    head1 = jnp.where(row8 == 0, p1, r1[:s])
    head2 = jnp.where(row8 == 0, p2, jnp.where(row8 == 1, p1, r2[:s]))
    return (jnp.concatenate([head1, r1[s:]], axis=0), jnp.concatenate([head2, r2[s:]], axis=0))


def _ffn_kernel(x_ref, wa_ref, wg_ref, cw_ref, wout_ref, g_ref, b_ref, o_ref, cs_ref,
                h_ref, carry_ref, *, tm, nsplit):
    i = pl.program_id(1)
    last = pl.num_programs(1) - 1

    @pl.when(i == 0)
    def _():
        carry_ref[...] = jnp.zeros_like(carry_ref)

    rg = tm // nsplit
    row8 = lax.broadcasted_iota(jnp.int32, (V7X_SUBLANES, FFN_CHUNK), 0)
    xbs = [x_ref[r * rg:(r + 1) * rg, :].astype(BF16) for r in range(nsplit)]
    for j in range(FFN_NCHUNK):
        cw = cw_ref[j]
        prev = carry_ref[j]
        p1 = prev[7:8, :]
        p2 = prev[6:7, :]
        for r in range(nsplit):
            a = jnp.dot(xbs[r], wa_ref[j], preferred_element_type=F32)
            gt = jnp.dot(xbs[r], wg_ref[j], preferred_element_type=F32)
            s1, s2 = _shift_rows(a, p1, p2, row8)
            h_ref[r * rg:(r + 1) * rg, j * FFN_CHUNK:(j + 1) * FFN_CHUNK] = _conv_gate(
                a, gt, s1, s2, cw).astype(BF16)
            p1 = a[rg - 1:rg, :]
            p2 = a[rg - 2:rg - 1, :]
        carry_ref[j] = a[rg - V7X_SUBLANES:, :]
    for r in range(nsplit):
        rows = slice(r * rg, (r + 1) * rg)
        out = jnp.dot(h_ref[rows, :], wout_ref[...], preferred_element_type=F32)
        o_ref[rows, :] = _deepnorm_ln(x_ref[rows, :], out, g_ref[...], b_ref[...])

    @pl.when(i == last)
    def _():
        cs_ref[...] = carry_ref[:, V7X_SUBLANES - (CONV_WIDTH - 1):, :]


def _ffn(x, wa, wg, cw, wout, g, bta, *, tm):
    b, t, _ = x.shape
    grid = (b, t // tm)
    tok = pl.BlockSpec((None, tm, D_MODEL), lambda bi, i: (bi, i, 0))
    cs_shape = (FFN_NCHUNK, CONV_WIDTH - 1, FFN_CHUNK)
    return pl.pallas_call(
        functools.partial(_ffn_kernel, tm=tm, nsplit=2),
        grid=grid,
        in_specs=[tok, _full(wa.shape), _full(wg.shape), _full(cw.shape), _full(wout.shape),
                  _full(g.shape), _full(bta.shape)],
        out_specs=[tok, pl.BlockSpec((None,) + cs_shape, lambda bi, i: (bi, 0, 0, 0))],
        out_shape=[
            jax.ShapeDtypeStruct((b, t, D_MODEL), F32),
            jax.ShapeDtypeStruct((b,) + cs_shape, F32),
        ],
        scratch_shapes=[
            pltpu.VMEM((tm, D_FF), BF16),
            pltpu.VMEM((FFN_NCHUNK, V7X_SUBLANES, FFN_CHUNK), F32),
        ],
        compiler_params=_params("arbitrary", "arbitrary"),
        name="conv_ffn",
    )(x, wa, wg, cw, wout, g, bta)


def _sample_attn_kernel(sink_ref, x_ref, wq_ref, wkv_ref, wo_ref, cos_ref, sin_ref, ck_ref, cv_ref,
                        g_ref, b_ref, *rest, bb):
    o_ref, kw_ref, vw_ref, q_scr, kv_scr, o_scr = rest[-6:]
    i = pl.program_id(0)
    last = pl.num_programs(0) - 1
    n = x_ref.shape[0]
    w = WINDOW

    @pl.when(i == 0)
    def _():
        xb = x_ref[...].astype(BF16)
        lane = lax.broadcasted_iota(jnp.int32, (n, V7X_LANES), 1)
        first_half = (lane & 63) < 32
        cos = cos_ref[...]
        sin = sin_ref[...]
        zq = jnp.dot(xb, wq_ref[...], preferred_element_type=F32)
        for c in range(NQ // 128):
            q_scr[:, c * 128:(c + 1) * 128] = _rope_128(
                zq[:, c * 128:(c + 1) * 128], cos, sin, first_half) * ATTN_SCALE
        zkv = jnp.dot(xb, wkv_ref[...], preferred_element_type=F32)
        for c in range(NKV // 128):
            kv_scr[:, c * 128:(c + 1) * 128] = _rope_128(
                zkv[:, c * 128:(c + 1) * 128], cos, sin, first_half)
        kv_scr[:, NKV:] = zkv[:, NKV:]

    r16 = lax.broadcasted_iota(jnp.int32, (ATTN_HEADS, NKV), 0)
    l16 = lax.broadcasted_iota(jnp.int32, (ATTN_HEADS, NKV), 1)
    diag = (l16 >> 6) == (r16 & (ATTN_KV_HEADS - 1))
    rowkey = lax.broadcasted_iota(jnp.int32, (w, NKV), 0)
    sink = sink_ref[...]

    def token(bi, carry):
        t = i * bb + bi
        qrow = q_scr[pl.ds(t, 1), :]
        knew = kv_scr[pl.ds(t, 1), 0:NKV]
        vnew = kv_scr[pl.ds(t, 1), NKV:]
        kc = ck_ref[bi]
        vc = cv_ref[bi]
        kw_ref[bi] = jnp.where(rowkey == w - 1, knew, pltpu.roll(kc, w - 1, 0))
        vw_ref[bi] = jnp.where(rowkey == w - 1, vnew, pltpu.roll(vc, w - 1, 0))
        qrep = jnp.broadcast_to(qrow[:, 3 * NKV:], (ATTN_HEADS, NKV))
        for hh in (2, 1, 0):
            qrep = jnp.where(r16 < (hh + 1) * ATTN_KV_HEADS,
                             jnp.broadcast_to(qrow[:, hh * NKV:(hh + 1) * NKV], (ATTN_HEADS, NKV)), qrep)
        qbd = jnp.where(diag, qrep, 0.0)
        s = lax.dot_general(qbd.astype(BF16), kc.astype(BF16), (((1,), (1,)), ((), ())),
                            preferred_element_type=F32)
        s_new = jnp.sum(qbd * knew, axis=-1, keepdims=True)
        m = jnp.maximum(jnp.maximum(jnp.max(s, axis=-1, keepdims=True), s_new), sink)
        e = jnp.exp(s - m)
        e_new = jnp.exp(s_new - m)
        denom = jnp.sum(e, axis=-1, keepdims=True) + e_new + jnp.exp(sink - m)
        o = jnp.dot(e.astype(BF16), vc.astype(BF16), preferred_element_type=F32) + e_new * vnew
        o = jnp.where(diag, o / denom, 0.0)
        for hh in range(ATTN_GROUP):
            part = jnp.where((r16 >> 2) == hh, o, 0.0)
            o_scr[pl.ds(t, 1), hh * NKV:(hh + 1) * NKV] = jnp.sum(part, axis=0, keepdims=True)
        return carry

    lax.fori_loop(0, bb, token, 0, unroll=2)

    @pl.when(i == last)
    def _():
        out = jnp.dot(o_scr[...].astype(BF16), wo_ref[...], preferred_element_type=F32)
        o_ref[...] = _deepnorm_ln(x_ref[...], out, g_ref[...], b_ref[...])


def _sample_attn(sinks_perm, x, wq_perm, wkv, wo_perm, cos, sin, cache_k_all, cache_v_all, g, bta,
                 prev_kv, *, layer, bb):
    n = x.shape[0]
    grid = (n // bb,)
    cache = pl.BlockSpec((None, bb, WINDOW, NKV), lambda i: (layer, i, 0, 0))
    args = [sinks_perm, x, wq_perm, wkv, wo_perm, cos, sin, cache_k_all, cache_v_all, g, bta]
    prev = list(prev_kv)
    prev_specs, aliases = _stacked_update(layer, prev, 1, len(args))
    return pl.pallas_call(
        functools.partial(_sample_attn_kernel, bb=bb),
        grid=grid,
        in_specs=[
            _full(sinks_perm.shape),
            _full(x.shape), _full(wq_perm.shape), _full(wkv.shape), _full(wo_perm.shape),
            _full(cos.shape), _full(sin.shape), cache, cache, _full(g.shape), _full(bta.shape),
        ] + prev_specs,
        out_specs=[_full((n, D_MODEL)), cache, cache],
        out_shape=[
            jax.ShapeDtypeStruct((n, D_MODEL), F32),
            jax.ShapeDtypeStruct(cache_k_all.shape, F32),
            jax.ShapeDtypeStruct(cache_v_all.shape, F32),
        ],
        scratch_shapes=[
            pltpu.VMEM((n, NQ), F32),
            pltpu.VMEM((n, 2 * NKV), F32),
            pltpu.VMEM((n, NQ), F32),
        ],
        input_output_aliases=aliases,
        compiler_params=_params("arbitrary"),
        name="sample_attn",
    )(*args, *prev)


def _sample_ret_in_kernel(x_ref, w_ref, cos_ref, sin_ref, qt_ref, kt_ref, v_ref, sg_ref):
    n = x_ref.shape[0]
    xb = x_ref[...].astype(BF16)
    lane = lax.broadcasted_iota(jnp.int32, (n, V7X_LANES), 1)
    even = (lane & 1) == 0
    for sec, out_ref, scale in ((0, qt_ref, None), (1, kt_ref, RET_K_SCALE)):
        z = jnp.dot(xb, w_ref[:, sec * RET_NQK:(sec + 1) * RET_NQK], preferred_element_type=F32)
        for c in range(RET_NQK // 128):
            t0 = (c % 2) * 128
            zc = _pair_rot_128(z[:, c * 128:(c + 1) * 128], cos_ref[:, t0:t0 + 128],
                               sin_ref[:, t0:t0 + 128], even)
            if scale is not None:
                zc = zc * scale
            out_ref[c * 128:(c + 1) * 128, :] = zc.T
    base = 2 * RET_NQK
    v_ref[...] = jnp.dot(xb, w_ref[:, base:base + RET_NV], preferred_element_type=F32)
    base = 2 * RET_NQK + RET_NV
    sg_ref[...] = _silu(jnp.dot(xb, w_ref[:, base:base + RET_NV], preferred_element_type=F32))


def _sample_ret_in(x, w_in_bf16, cos, sin):
    n = x.shape[0]
    return pl.pallas_call(
        _sample_ret_in_kernel,
        grid=(1,),
        in_specs=[_full(x.shape), _full(w_in_bf16.shape), _full(cos.shape), _full(sin.shape)],
        out_specs=[_full((RET_NQK, n)), _full((RET_NQK, n)), _full((n, RET_NV)), _full((n, RET_NV))],
        out_shape=[
            jax.ShapeDtypeStruct((RET_NQK, n), F32),
            jax.ShapeDtypeStruct((RET_NQK, n), F32),
            jax.ShapeDtypeStruct((n, RET_NV), F32),
            jax.ShapeDtypeStruct((n, RET_NV), F32),
        ],
        compiler_params=_params("arbitrary"),
        name="sample_ret_in",
    )(x, w_in_bf16, cos, sin)


def _sample_ret_state_kernel(gam_ref, qt_ref, kt_ref, v_ref, s_ref, *rest, bb):
    o_ref, sn_ref = rest[-2:]
    i = pl.program_id(0)
    n = v_ref.shape[0]
    lane = lax.broadcasted_iota(jnp.int32, (1, n), 1)
    for bi in range(bb):
        t = i * bb + bi
        onehot = (lane == t).astype(F32)
        for h in range(RET_HEADS):
            krows = slice(h * RET_KEY_DIM, (h + 1) * RET_KEY_DIM)
            vcols = slice(h * RET_VALUE_DIM, (h + 1) * RET_VALUE_DIM)
            qc = jnp.sum(qt_ref[krows, :] * onehot, axis=1, keepdims=True)
            kc = jnp.sum(kt_ref[krows, :] * onehot, axis=1, keepdims=True)
            vrow = v_ref[pl.ds(t, 1), vcols]
            s = s_ref[bi, h]
            gamma = gam_ref[h]
            cross = jnp.sum(s * qc, axis=0, keepdims=True)
            qk = jnp.sum(qc * kc, axis=0, keepdims=True)
            o_ref[pl.ds(t, 1), vcols] = qk * vrow + cross * gamma
            sn_ref[bi, h] = s * gamma + kc * vrow


def _stacked_update(layer, prev_outs, first_out_index, n_inputs):
    specs = [pl.BlockSpec(memory_space=pl.ANY) for _ in prev_outs]
    aliases = {n_inputs + k: first_out_index + k for k in range(len(prev_outs))}
    return specs, aliases


def _sample_ret_state(gamma, qt, kt, v, state_all, prev_out, *, layer, bb):
    n = v.shape[0]
    st = pl.BlockSpec((None, bb, RET_HEADS, RET_KEY_DIM, RET_VALUE_DIM),
                      lambda i: (layer, i, 0, 0, 0))
    args = [gamma, qt, kt, v, state_all]
    prev = [] if prev_out is None else [prev_out]
    prev_specs, aliases = _stacked_update(layer, prev, 1, len(args))
    return pl.pallas_call(
        functools.partial(_sample_ret_state_kernel, bb=bb),
        grid=(n // bb,),
        in_specs=[pl.BlockSpec(memory_space=pltpu.SMEM), _full(qt.shape), _full(kt.shape),
                  _full(v.shape), st] + prev_specs,
        out_specs=[_full((n, RET_NV)), st],
        out_shape=[
            jax.ShapeDtypeStruct((n, RET_NV), F32),
            jax.ShapeDtypeStruct(state_all.shape, F32),
        ],
        input_output_aliases=aliases,
        compiler_params=_params("arbitrary"),
        name="sample_ret_state",
    )(*args, *prev)


def _sample_ret_out_kernel(o_ref, sg_ref, x_ref, wo_ref, g_ref, b_ref, y_ref):
    n = x_ref.shape[0]
    acc = jnp.zeros((n, D_MODEL), F32)
    for h in range(RET_HEADS):
        vcols = slice(h * RET_VALUE_DIM, (h + 1) * RET_VALUE_DIM)
        o = o_ref[:, vcols]
        mu = jnp.mean(o, axis=-1, keepdims=True)
        d = o - mu
        var = jnp.mean(d * d, axis=-1, keepdims=True)
        gated = (sg_ref[:, vcols] * (d * lax.rsqrt(var + GN_EPS))).astype(BF16)
        acc = acc + jnp.dot(gated, wo_ref[vcols, :], preferred_element_type=F32)
    y_ref[...] = _deepnorm_ln(x_ref[...], acc, g_ref[...], b_ref[...])


def _sample_ret_out(o, sg, x, wo_bf16, g, bta):
    args = (o, sg, x, wo_bf16, g, bta)
    return pl.pallas_call(
        _sample_ret_out_kernel,
        grid=(1,),
        in_specs=[_full(a.shape) for a in args],
        out_specs=_full(x.shape),
        out_shape=jax.ShapeDtypeStruct(x.shape, F32),
        compiler_params=_params("arbitrary"),
        name="sample_ret_out",
    )(*args)


def _sample_ffn_kernel(x_ref, b0_ref, b1_ref, wa_ref, wg_ref, cw_ref, wout_ref, g_ref, b_ref,
                       y_ref, a_ref):
    n = x_ref.shape[0]
    xb = x_ref[...].astype(BF16)
    acc = jnp.zeros((n, D_MODEL), F32)
    for j in range(FFN_NCHUNK):
        cols = slice(j * FFN_CHUNK, (j + 1) * FFN_CHUNK)
        a = jnp.dot(xb, wa_ref[j], preferred_element_type=F32)
        gt = jnp.dot(xb, wg_ref[j], preferred_element_type=F32)
        h = _conv_gate(a, gt, b1_ref[:, cols], b0_ref[:, cols], cw_ref[j])
        a_ref[:, cols] = a
        acc = acc + jnp.dot(h.astype(BF16), wout_ref[cols, :], preferred_element_type=F32)
    y_ref[...] = _deepnorm_ln(x_ref[...], acc, g_ref[...], b_ref[...])


def _sample_ffn(x, buf0, buf1, wa, wg, cw, wout, g, bta):
    args = (x, buf0, buf1, wa, wg, cw, wout, g, bta)
    return pl.pallas_call(
        _sample_ffn_kernel,
        grid=(1,),
        in_specs=[_full(a.shape) for a in args],
        out_specs=[_full(x.shape), _full(buf0.shape)],
        out_shape=[jax.ShapeDtypeStruct(x.shape, F32), jax.ShapeDtypeStruct(buf0.shape, F32)],
        compiler_params=_params("arbitrary"),
        name="sample_ffn",
    )(*args)


def _head_major_to_group_minor(w, axis):
    shape = w.shape
    split = shape[:axis] + (ATTN_KV_HEADS, ATTN_GROUP, ATTN_HEAD_DIM) + shape[axis + 1:]
    return jnp.swapaxes(w.reshape(split), axis, axis + 1).reshape(shape)


def kernel(x_prompt, x_sample, cache_k_win, cache_v_win, state_ret, state_conv, attn_w_qkv, attn_sinks, attn_w_o, ret_w_in, ret_w_o, ffn_w_in, ffn_conv_w, ffn_conv_b, ffn_w_out, ln_mix_g, ln_mix_b, ln_ffn_g, ln_ffn_b):
    bp, tp, _ = x_prompt.shape
    ns = x_sample.shape[0]
    assert x_sample.shape[1] == 1, "the sample group carries one new token per sequence"
    xp = x_prompt
    xs = x_sample.reshape(ns, D_MODEL)
    pos_p = jnp.arange(tp)
    pos_s = jnp.full((ns,), PAST_LEN, jnp.int32)
    rope_p = _rope_tables(pos_p)
    rope_s = _rope_tables(pos_s)
    rot_p = _ret_rot_tables(pos_p)
    rot_s = _ret_rot_tables(pos_s)
    decay_in, q_decay, k_decay, chunk_decay = _ret_decay_tables(RET_CHUNK)
    gamma = jnp.exp(_ret_log_gamma())
    row = lambda v: v.reshape(1, D_MODEL)

    tm = min(512, tp)
    cache_k_all = cache_k_win.reshape(cache_k_win.shape[0], ns, WINDOW, NKV)
    cache_v_all = cache_v_win.reshape(cache_v_win.shape[0], ns, WINDOW, NKV)
    kv_s = ()
    rs = None
    kp_l, vp_l, rp_l, cp_l, cs_l = [], [], [], [], []
    for i in range(DEPTH):
        j = i // N_MIXERS
        g_mix, b_mix = row(ln_mix_g[i]), row(ln_mix_b[i])
        if i % N_MIXERS == 0:
            w_qkv = attn_w_qkv[j].astype(BF16)
            w_o = attn_w_o[j].astype(BF16)
            q, k2, v2, kf, vf = _attn_qkv(xp, w_qkv, *rope_p, tm=tm)
            xp = _attn_core(attn_sinks[j], q, k2, v2, xp, w_o, g_mix, b_mix, tq=tm)
            kp_l.append(kf.reshape(bp, WINDOW, ATTN_KV_HEADS, ATTN_HEAD_DIM))
            vp_l.append(vf.reshape(bp, WINDOW, ATTN_KV_HEADS, ATTN_HEAD_DIM))
            xs, *kv_s = _sample_attn(
                attn_sinks[j].reshape(ATTN_KV_HEADS, ATTN_GROUP).T.reshape(ATTN_HEADS, 1), xs,
                _head_major_to_group_minor(w_qkv[:, :NQ], 1), w_qkv[:, NQ:],
                _head_major_to_group_minor(w_o, 0), *rope_s, cache_k_all, cache_v_all,
                g_mix, b_mix, kv_s, layer=j, bb=min(16, ns))
        else:
            w_in = ret_w_in[j].astype(BF16)
            w_o = ret_w_o[j].astype(BF16)
            q, k, v, sg = _ret_in(xp, w_in, *rot_p, tm=tm)
            xp, rp = _ret_core(chunk_decay, q, k, v, sg, xp, decay_in, q_decay, k_decay, w_o,
                               g_mix, b_mix, ct=tm)
            rp_l.append(rp)
            qt, kt, vs, sgs = _sample_ret_in(xs, w_in, *rot_s)
            os_, rs = _sample_ret_state(gamma, qt, kt, vs, state_ret, rs, layer=j, bb=2)
            xs = _sample_ret_out(os_, sgs, xs, w_o, g_mix, b_mix)
        w_in = ffn_w_in[i]
        chunked = lambda w: w.reshape(D_MODEL, FFN_NCHUNK, FFN_CHUNK).swapaxes(0, 1).astype(BF16)
        wa, wg = chunked(w_in[:, :D_FF]), chunked(w_in[:, D_FF:])
        wout = ffn_w_out[i].astype(BF16)
        cw = jnp.concatenate(
            [ffn_conv_w[i], ffn_conv_b[i][None], jnp.zeros((V7X_SUBLANES - CONV_WIDTH - 1, D_FF), F32)], axis=0)
        cw = cw.reshape(V7X_SUBLANES, FFN_NCHUNK, FFN_CHUNK).swapaxes(0, 1)
        g_ffn, b_ffn = row(ln_ffn_g[i]), row(ln_ffn_b[i])
        xp, cp = _ffn(xp, wa, wg, cw, wout, g_ffn, b_ffn, tm=tm)
        cp_l.append(cp.swapaxes(1, 2).reshape(bp, CONV_WIDTH - 1, D_FF))
        buf = state_conv[i]
        xs, a_new = _sample_ffn(xs, buf[:, 0, :], buf[:, 1, :], wa, wg, cw, wout, g_ffn, b_ffn)
        cs_l.append(jnp.stack([buf[:, 1, :], a_new], axis=1))
    ks, vs = (w.reshape(cache_k_win.shape) for w in kv_s)
    return (xp, xs.reshape(ns, 1, D_MODEL),
            jnp.stack(kp_l), jnp.stack(vp_l), jnp.stack(rp_l), jnp.stack(cp_l),
            ks, vs, rs, jnp.stack(cs_l))
```

```python
import functools
from typing import NamedTuple, Optional

import jax
import jax.numpy as jnp
from jax import lax
from jax.experimental import pallas as pl
from jax.experimental.pallas import tpu as pltpu

F32 = jnp.float32
BF16 = jnp.bfloat16

D_MODEL = 1024
DEPTH = 4
PAST_LEN = 8192
N_MIXERS = 2
ATTN_HEAD_DIM = 64
ATTN_HEADS = 16
ATTN_KV_HEADS = 4
ATTN_GROUP = ATTN_HEADS // ATTN_KV_HEADS
WINDOW = 128
ROPE_THETA = 10000.0
RET_KEY_DIM = 256
RET_HEADS = 4
RET_VALUE_DIM = 512
RET_CHUNK = 128
D_FF = 2816
CONV_WIDTH = 3
LN_EPS = 1e-5
GN_EPS = 1e-5
DEEPNORM_ALPHA = (2.0 * DEPTH) ** 0.25
ATTN_SCALE = ATTN_HEAD_DIM ** -0.5
RET_K_SCALE = RET_KEY_DIM ** -0.5

V7X_LANES = 128
V7X_SUBLANES = 8
V7X_VMEM_LIMIT_BYTES = 56 * 1024 * 1024

NQ = ATTN_HEADS * ATTN_HEAD_DIM
NKV = ATTN_KV_HEADS * ATTN_HEAD_DIM
RET_NQK = RET_HEADS * RET_KEY_DIM
RET_NV = RET_HEADS * RET_VALUE_DIM
FFN_CHUNK = 256
FFN_NCHUNK = D_FF // FFN_CHUNK
MASKED_SCORE = -1e30


def _params(*semantics, flags=None):
    return pltpu.CompilerParams(
        dimension_semantics=semantics, vmem_limit_bytes=V7X_VMEM_LIMIT_BYTES, flags=flags
    )


def _full(shape):
    n = len(shape)
    return pl.BlockSpec(shape, lambda *_: (0,) * n)


def _deepnorm_ln(x, m, g, b):
    y = DEEPNORM_ALPHA * x + m
    mu = jnp.mean(y, axis=-1, keepdims=True)
    d = y - mu
    var = jnp.mean(d * d, axis=-1, keepdims=True)
    return d * lax.rsqrt(var + LN_EPS) * g + b


def _silu(z):
    return z * jax.nn.sigmoid(z)


def _rope_tables(pos):
    half = ATTN_HEAD_DIM // 2
    lane = jnp.arange(V7X_LANES)
    inv_freq = ROPE_THETA ** (-(lane % half).astype(F32) / half)
    sign = jnp.where((lane % ATTN_HEAD_DIM) < half, -1.0, 1.0).astype(F32)
    ang = pos.astype(F32)[:, None] * inv_freq[None, :]
    return jnp.cos(ang), jnp.sin(ang) * sign[None, :]


def _ret_rot_tables(pos):
    d = RET_KEY_DIM
    angle = jnp.repeat(1.0 / (10000.0 ** jnp.linspace(0.0, 1.0, d // 2, dtype=F32)), 2)
    sign = jnp.where(jnp.arange(d) % 2 == 0, -1.0, 1.0).astype(F32)
    ang = pos.astype(F32)[:, None] * angle[None, :]
    return jnp.cos(ang), jnp.sin(ang) * sign[None, :]


def _ret_log_gamma():
    return jnp.log(1.0 - 2.0 ** (-5.0 - jnp.arange(RET_HEADS, dtype=F32)))


def _ret_decay_tables(chunk):
    log_gamma = _ret_log_gamma()
    idx = jnp.arange(chunk, dtype=F32)
    rel = idx[:, None] - idx[None, :]
    decay_in = jnp.where(
        rel[None] >= 0,
        jnp.exp(log_gamma[:, None, None] * jnp.maximum(rel, 0.0)[None]),
        0.0,
    )
    q_decay = jnp.exp(log_gamma[:, None] * (idx[None, :] + 1.0))
    k_decay = jnp.exp(log_gamma[:, None] * (chunk - 1.0 - idx[None, :]))
    chunk_decay = jnp.exp(log_gamma * chunk)
    lane_bcast = lambda t: jnp.broadcast_to(t[:, :, None], (RET_HEADS, chunk, V7X_LANES))
    return decay_in, lane_bcast(q_decay), lane_bcast(k_decay), chunk_decay


def _rope_128(z, cos, sin_signed, first_half):
    partner = jnp.where(first_half, pltpu.roll(z, 96, 1), pltpu.roll(z, 32, 1))
    return z * cos + partner * sin_signed


def _pair_rot_128(z, cos, sin_signed, even_lane):
    partner = jnp.where(even_lane, pltpu.roll(z, 127, 1), pltpu.roll(z, 1, 1))
    return z * cos + partner * sin_signed


def _attn_qkv_kernel(x_ref, w_ref, cos_ref, sin_ref, q_ref, k2_ref, v2_ref, kf_ref, vf_ref, *, tm):
    i = pl.program_id(1)
    last = pl.num_programs(1) - 1
    xb = x_ref[...].astype(BF16)
    cos = cos_ref[...]
    sin = sin_ref[...]
    lane = lax.broadcasted_iota(jnp.int32, (tm, V7X_LANES), 1)
    first_half = (lane & 63) < 32
    lo = lane < 64

    def dup(z, take_low):
        zr = pltpu.roll(z, 64, 1)
        return jnp.where(lo, z, zr) if take_low else jnp.where(lo, zr, z)

    for j in range(NQ // 512):
        z = jnp.dot(xb, w_ref[:, j * 512:(j + 1) * 512], preferred_element_type=F32)
        for c in range(4):
            zc = _rope_128(z[:, c * 128:(c + 1) * 128], cos, sin, first_half) * ATTN_SCALE
            q_ref[:, j * 512 + c * 128:j * 512 + (c + 1) * 128] = zc.astype(BF16)
    zkv = jnp.dot(xb, w_ref[:, NQ:NQ + 2 * NKV], preferred_element_type=F32)
    for c in range(NKV // 128):
        kc = _rope_128(zkv[:, c * 128:(c + 1) * 128], cos, sin, first_half)
        vc = zkv[:, NKV + c * 128:NKV + (c + 1) * 128]
        for gg in range(2):
            g = 2 * c + gg
            k2_ref[:, g * 128:(g + 1) * 128] = dup(kc, gg == 0).astype(BF16)
            v2_ref[:, g * 128:(g + 1) * 128] = dup(vc, gg == 0).astype(BF16)

        @pl.when(i == last)
        def _():
            kf_ref[:, c * 128:(c + 1) * 128] = kc[tm - WINDOW:, :]
            vf_ref[:, c * 128:(c + 1) * 128] = vc[tm - WINDOW:, :]


def _attn_qkv(x, w_qkv_bf16, cos, sin, *, tm):
    b, t, _ = x.shape
    grid = (b, t // tm)
    tok = lambda w: pl.BlockSpec((None, tm, w), lambda bi, i: (bi, i, 0))
    tab = pl.BlockSpec((tm, V7X_LANES), lambda bi, i: (i, 0))
    tail = pl.BlockSpec((None, WINDOW, NKV), lambda bi, i: (bi, 0, 0))
    return pl.pallas_call(
        functools.partial(_attn_qkv_kernel, tm=tm),
        grid=grid,
        in_specs=[tok(D_MODEL), _full(w_qkv_bf16.shape), tab, tab],
        out_specs=[tok(NQ), tok(2 * NKV), tok(2 * NKV), tail, tail],
        out_shape=[
            jax.ShapeDtypeStruct((b, t, NQ), BF16),
            jax.ShapeDtypeStruct((b, t, 2 * NKV), BF16),
            jax.ShapeDtypeStruct((b, t, 2 * NKV), BF16),
            jax.ShapeDtypeStruct((b, WINDOW, NKV), F32),
            jax.ShapeDtypeStruct((b, WINDOW, NKV), F32),
        ],
        compiler_params=_params("arbitrary", "arbitrary"),
        name="attn_qkv",
    )(x, w_qkv_bf16, cos, sin)


def _attn_core_kernel(sink_ref, q_ref, kc_ref, kp_ref, vc_ref, vp_ref, x_ref, wo_ref, g_ref, b_ref,
                      o_ref, kcat, vcat, oscr, *, tq):
    i = pl.program_id(1)
    kcat[0:WINDOW, :] = kp_ref[...]
    kcat[WINDOW:, :] = kc_ref[...]
    vcat[0:WINDOW, :] = vp_ref[...]
    vcat[WINDOW:, :] = vc_ref[...]
    w = WINDOW
    ii = lax.broadcasted_iota(jnp.int32, (2 * w, 2 * w), 0) & (w - 1)
    jj = lax.broadcasted_iota(jnp.int32, (2 * w, 2 * w), 1)
    band = (jj >= ii) & (jj <= ii + w)
    first_key = jnp.where(i == 0, w, 0)
    lo = lax.broadcasted_iota(jnp.int32, (w, V7X_LANES), 1) < 64
    top = lax.broadcasted_iota(jnp.int32, (2 * w, 1), 0) < w
    zero = jnp.zeros((w, V7X_LANES), BF16)
    for qb in range(tq // w):
        mask = band & (jj >= first_key) if qb == 0 else band
        rows = slice(qb * w, (qb + 1) * w)
        keys = slice(qb * w, (qb + 2) * w)
        for p in range(ATTN_HEADS // 2):
            g = p // (ATTN_GROUP // 2)
            cols = slice(p * 128, (p + 1) * 128)
            gcols = slice(g * 128, (g + 1) * 128)
            q2 = q_ref[rows, cols]
            qs = jnp.concatenate([jnp.where(lo, q2, zero), jnp.where(lo, zero, q2)], axis=0)
            s = lax.dot_general(qs, kcat[keys, gcols], (((1,), (1,)), ((), ())),
                                preferred_element_type=F32)
            s = jnp.where(mask, s, MASKED_SCORE)
            sink = jnp.where(top, sink_ref[2 * p], sink_ref[2 * p + 1])
            m = jnp.maximum(jnp.max(s, axis=-1, keepdims=True), sink)
            e = jnp.exp(s - m)
            denom = jnp.sum(e, axis=-1, keepdims=True) + jnp.exp(sink - m)
            o2 = jnp.dot(e.astype(BF16), vcat[keys, gcols], preferred_element_type=F32)
            o2 = o2 / denom
            oscr[rows, cols] = jnp.where(lo, o2[:w], o2[w:]).astype(BF16)
    out = jnp.dot(oscr[...], wo_ref[...], preferred_element_type=F32)
    o_ref[...] = _deepnorm_ln(x_ref[...], out, g_ref[...], b_ref[...])


def _attn_core(sinks, q, k2, v2, x, wo_bf16, g, bta, *, tq):
    b, t, _ = x.shape
    grid = (b, t // tq)
    r = tq // WINDOW
    cur = lambda w: pl.BlockSpec((None, tq, w), lambda bi, i: (bi, i, 0))
    prev = pl.BlockSpec((None, WINDOW, 2 * NKV), lambda bi, i: (bi, jnp.maximum(i * r - 1, 0), 0))
    return pl.pallas_call(
        functools.partial(_attn_core_kernel, tq=tq),
        grid=grid,
        in_specs=[
            pl.BlockSpec(memory_space=pltpu.SMEM),
            cur(NQ), cur(2 * NKV), prev, cur(2 * NKV), prev, cur(D_MODEL),
            _full(wo_bf16.shape), _full(g.shape), _full(bta.shape),
        ],
        out_specs=cur(D_MODEL),
        out_shape=jax.ShapeDtypeStruct((b, t, D_MODEL), F32),
        scratch_shapes=[
            pltpu.VMEM((tq + WINDOW, 2 * NKV), BF16),
            pltpu.VMEM((tq + WINDOW, 2 * NKV), BF16),
            pltpu.VMEM((tq, NQ), BF16),
        ],
        compiler_params=_params("arbitrary", "arbitrary"),
        name="attn_core",
    )(sinks, q, k2, k2, v2, v2, x, wo_bf16, g, bta)


def _ret_in_kernel(x_ref, w_ref, cos_ref, sin_ref, q_ref, k_ref, v_ref, sg_ref, *, tm):
    xb = x_ref[...].astype(BF16)
    lane = lax.broadcasted_iota(jnp.int32, (tm, V7X_LANES), 1)
    even = (lane & 1) == 0
    nc = 512
    for sec, out_ref, scale in ((0, q_ref, None), (1, k_ref, RET_K_SCALE)):
        for j in range(RET_NQK // nc):
            z = jnp.dot(xb, w_ref[:, sec * RET_NQK + j * nc:sec * RET_NQK + (j + 1) * nc],
                        preferred_element_type=F32)
            for c in range(nc // 128):
                t0 = (c % 2) * 128
                zc = _pair_rot_128(z[:, c * 128:(c + 1) * 128], cos_ref[:, t0:t0 + 128],
                                   sin_ref[:, t0:t0 + 128], even)
                if scale is not None:
                    zc = zc * scale
                out_ref[:, j * nc + c * 128:j * nc + (c + 1) * 128] = zc.astype(BF16)
    base = 2 * RET_NQK
    for j in range(RET_NV // nc):
        z = jnp.dot(xb, w_ref[:, base + j * nc:base + (j + 1) * nc], preferred_element_type=F32)
        v_ref[:, j * nc:(j + 1) * nc] = z.astype(BF16)
    base = 2 * RET_NQK + RET_NV
    for j in range(RET_NV // nc):
        z = jnp.dot(xb, w_ref[:, base + j * nc:base + (j + 1) * nc], preferred_element_type=F32)
        sg_ref[:, j * nc:(j + 1) * nc] = _silu(z).astype(BF16)


def _ret_in(x, w_in_bf16, cos, sin, *, tm):
    b, t, _ = x.shape
    grid = (b, t // tm)
    tok = lambda w: pl.BlockSpec((None, tm, w), lambda bi, i: (bi, i, 0))
    tab = pl.BlockSpec((tm, RET_KEY_DIM), lambda bi, i: (i, 0))
    return pl.pallas_call(
        functools.partial(_ret_in_kernel, tm=tm),
        grid=grid,
        in_specs=[tok(D_MODEL), _full(w_in_bf16.shape), tab, tab],
        out_specs=[tok(RET_NQK), tok(RET_NQK), tok(RET_NV), tok(RET_NV)],
        out_shape=[
            jax.ShapeDtypeStruct((b, t, RET_NQK), BF16),
            jax.ShapeDtypeStruct((b, t, RET_NQK), BF16),
            jax.ShapeDtypeStruct((b, t, RET_NV), BF16),
            jax.ShapeDtypeStruct((b, t, RET_NV), BF16),
        ],
        compiler_params=_params("arbitrary", "arbitrary"),
        name="ret_in",
    )(x, w_in_bf16, cos, sin)


def _ret_core_kernel(cd_ref, q_ref, k_ref, v_ref, sg_ref, x_ref, din_ref, qd_ref, kd_ref, wo_ref,
                     g_ref, b_ref, *rest, ct, rider):
    nr = rider.n_in if rider else 0
    o_ref, st_ref = rest[nr:nr + 2]
    s_scr, gated_scr = rest[-2:]
    i = pl.program_id(1)
    last = pl.num_programs(1) - 1
    c = RET_CHUNK
    nchunk = ct // c
    step = pl.program_id(0) * pl.num_programs(1) + i
    rider_items = rider.items(rest[:nr], rest[nr + 2:-2], step) if rider else []
    per_chunk = -(-len(rider_items) // nchunk)

    @pl.when(i == 0)
    def _():
        s_scr[...] = jnp.zeros_like(s_scr)

    heads = range(RET_HEADS)
    kcols = [slice(h * RET_KEY_DIM, (h + 1) * RET_KEY_DIM) for h in heads]
    vcols = [slice(h * RET_VALUE_DIM, (h + 1) * RET_VALUE_DIM) for h in heads]
    nt = (((1,), (1,)), ((), ()))
    tn = (((0,), (0,)), ((), ()))
    for ci in range(ct // c):
        rows = slice(ci * c, (ci + 1) * c)
        qs = [q_ref[rows, kcols[h]] for h in heads]
        ks = [k_ref[rows, kcols[h]] for h in heads]
        vs = [v_ref[rows, vcols[h]] for h in heads]
        inner = [lax.dot_general(qs[h], ks[h], nt, preferred_element_type=F32) for h in heads]
        inner = [(inner[h] * din_ref[h]).astype(BF16) for h in heads]
        o = []
        for h in heads:
            qd = jnp.concatenate([qd_ref[h]] * (RET_KEY_DIM // V7X_LANES), axis=1)
            qdec = (qs[h].astype(F32) * qd).astype(BF16)
            o.append(jnp.dot(inner[h], vs[h], preferred_element_type=F32)
                     + jnp.dot(qdec, s_scr[h].astype(BF16), preferred_element_type=F32))
        for h in heads:
            kd = jnp.concatenate([kd_ref[h]] * (RET_KEY_DIM // V7X_LANES), axis=1)
            kdec = (ks[h].astype(F32) * kd).astype(BF16)
            s_scr[h] = s_scr[h] * cd_ref[h] + lax.dot_general(
                kdec, vs[h], tn, preferred_element_type=F32)
        for item in rider_items[ci * per_chunk:(ci + 1) * per_chunk]:
            item()
        for h in heads:
            mu = jnp.mean(o[h], axis=-1, keepdims=True)
            d = o[h] - mu
            var = jnp.mean(d * d, axis=-1, keepdims=True)
            on = d * lax.rsqrt(var + GN_EPS)
            gated_scr[rows, vcols[h]] = (sg_ref[rows, vcols[h]].astype(F32) * on).astype(BF16)
        out = jnp.dot(gated_scr[rows, :], wo_ref[...], preferred_element_type=F32)
        o_ref[rows, :] = _deepnorm_ln(x_ref[rows, :], out, g_ref[...], b_ref[...])

    @pl.when(i == last)
    def _():
        st_ref[...] = s_scr[...]


def _ret_core(chunk_decay, q, k, v, sg, x, decay_in, q_decay, k_decay, wo_bf16, g, bta, rider, *, ct):
    b, t, _ = x.shape
    grid = (b, t // ct)
    tok = lambda w: pl.BlockSpec((None, ct, w), lambda bi, i: (bi, i, 0))
    st_shape = (RET_HEADS, RET_KEY_DIM, RET_VALUE_DIM)
    args = [chunk_decay, q, k, v, sg, x, decay_in, q_decay, k_decay, wo_bf16, g, bta]
    in_specs = [
        pl.BlockSpec(memory_space=pltpu.SMEM),
        tok(RET_NQK), tok(RET_NQK), tok(RET_NV), tok(RET_NV), tok(D_MODEL),
        _full(decay_in.shape), _full(q_decay.shape), _full(k_decay.shape),
        _full(wo_bf16.shape), _full(g.shape), _full(bta.shape),
    ]
    out_specs = [tok(D_MODEL), pl.BlockSpec((None,) + st_shape, lambda bi, i: (bi, 0, 0, 0))]
    out_shape = [jax.ShapeDtypeStruct((b, t, D_MODEL), F32), jax.ShapeDtypeStruct((b,) + st_shape, F32)]
    aliases = {}
    plan = None
    if rider is not None:
        n_steps = grid[0] * grid[1]
        plan = rider.plan(n_steps)
        r_args, r_in, r_out_specs, r_out_shape, aliases = rider.specs(
            n_steps, lambda bi, i: bi * grid[1] + i, len(args), len(out_shape))
        args += r_args
        in_specs += r_in
        out_specs += r_out_specs
        out_shape += r_out_shape
    return pl.pallas_call(
        functools.partial(_ret_core_kernel, ct=ct, rider=plan),
        grid=grid,
        in_specs=in_specs,
        out_specs=out_specs,
        out_shape=out_shape,
        scratch_shapes=[pltpu.VMEM(st_shape, F32), pltpu.VMEM((ct, RET_NV), BF16)],
        input_output_aliases=aliases,
        compiler_params=_params("arbitrary", "arbitrary"),
        name="ret_core",
    )(*args)


def _ret_state_item(gam_ref, qt_ref, kt_ref, v_ref, s_ref, o_ref, sn_ref, onehot, t, row, bi, h):
    krows = slice(h * RET_KEY_DIM, (h + 1) * RET_KEY_DIM)
    vcols = slice(h * RET_VALUE_DIM, (h + 1) * RET_VALUE_DIM)
    qc = jnp.sum(qt_ref[krows, :] * onehot, axis=1, keepdims=True)
    kc = jnp.sum(kt_ref[krows, :] * onehot, axis=1, keepdims=True)
    vrow = v_ref[pl.ds(t, 1), vcols]
    s = s_ref[bi, h]
    gamma = gam_ref[h]
    cross = jnp.sum(s * qc, axis=0, keepdims=True)
    qk = jnp.sum(qc * kc, axis=0, keepdims=True)
    o_ref[pl.ds(row, 1), vcols] = qk * vrow + cross * gamma
    sn_ref[bi, h] = s * gamma + kc * vrow


class _RiderPlan(NamedTuple):
    n_in: int
    first: int
    per_step: int

    def items(self, in_refs, out_refs, step):
        gam_ref, qt_ref, kt_ref, v_ref, s_ref = in_refs[:5]
        o_ref, sn_ref = out_refs
        lane = lax.broadcasted_iota(jnp.int32, (1, qt_ref.shape[1]), 1)
        items = []
        for bi in range(self.per_step):
            row = step * self.per_step + bi
            t = self.first + row
            onehot = (lane == t).astype(F32)
            for h in range(RET_HEADS):
                items.append(functools.partial(
                    _ret_state_item, gam_ref, qt_ref, kt_ref, v_ref, s_ref, o_ref, sn_ref,
                    onehot, t, row, bi, h))
        return items


class _RetStateRider(NamedTuple):
    gamma: jax.Array
    qt: jax.Array
    kt: jax.Array
    v: jax.Array
    state_all: jax.Array
    prev_out: Optional[jax.Array]
    layer: int
    first: int
    count: int

    def plan(self, n_steps):
        return _RiderPlan(5 + (self.prev_out is not None), self.first, self.count // n_steps)

    def specs(self, n_steps, flat_index, n_inputs, n_outputs):
        per_step = self.count // n_steps
        blk0 = self.first // per_step
        st = pl.BlockSpec((None, per_step, RET_HEADS, RET_KEY_DIM, RET_VALUE_DIM),
                          lambda *g: (self.layer, blk0 + flat_index(*g), 0, 0, 0))
        args = [self.gamma, self.qt, self.kt, self.v, self.state_all]
        in_specs = [pl.BlockSpec(memory_space=pltpu.SMEM), _full(self.qt.shape), _full(self.kt.shape),
                    _full(self.v.shape), st]
        aliases = {}
        if self.prev_out is not None:
            args.append(self.prev_out)
            in_specs.append(pl.BlockSpec(memory_space=pl.ANY))
            aliases = {n_inputs + 5: n_outputs + 1}
        out_specs = [_full((self.count, RET_NV)), st]
        out_shape = [jax.ShapeDtypeStruct((self.count, RET_NV), F32),
                     jax.ShapeDtypeStruct(self.state_all.shape, F32)]
        return args, in_specs, out_specs, out_shape, aliases


def _conv_gate(a, gt, s1, s2, cw):
    c = cw[3:4, :] + s2 * cw[0:1, :]
    c = c + s1 * cw[1:2, :]
    c = c + a * cw[2:3, :]
    return _silu(c) * gt


def _shift_rows(a, p1, p2, row8):
    r1 = pltpu.roll(a, 1, 0)
    r2 = pltpu.roll(a, 2, 0)
    s = V7X_SUBLANES
    head1 = jnp.where(row8 == 0, p1, r1[:s])
    head2 = jnp.where(row8 == 0, p2, jnp.where(row8 == 1, p1, r2[:s]))
    return (jnp.concatenate([head1, r1[s:]], axis=0), jnp.concatenate([head2, r2[s:]], axis=0))


def _ffn_kernel(x_ref, win_ref, cw_ref, wout_ref, g_ref, b_ref, *rest, tm, nsplit, rider):
    nr = rider.n_in if rider else 0
    o_ref, cs_ref = rest[nr:nr + 2]
    h_ref, carry_ref = rest[-2:]
    i = pl.program_id(1)
    last = pl.num_programs(1) - 1
    step = pl.program_id(0) * pl.num_programs(1) + i

    @pl.when(i == 0)
    def _():
        carry_ref[...] = jnp.zeros_like(carry_ref)

    rider_items = rider.items(rest[:nr], rest[nr + 2:-2], step) if rider else []
    rg = tm // nsplit
    row8 = lax.broadcasted_iota(jnp.int32, (V7X_SUBLANES, FFN_CHUNK), 0)
    xbs = [x_ref[r * rg:(r + 1) * rg, :].astype(BF16) for r in range(nsplit)]
    for j in range(FFN_NCHUNK):
        cols = slice(j * FFN_CHUNK, (j + 1) * FFN_CHUNK)
        gcols = slice(D_FF + j * FFN_CHUNK, D_FF + (j + 1) * FFN_CHUNK)
        cw = cw_ref[:, cols]
        prev = carry_ref[:, cols]
        p1 = prev[7:8, :]
        p2 = prev[6:7, :]
        for r in range(nsplit):
            a = jnp.dot(xbs[r], win_ref[:, cols], preferred_element_type=F32)
            gt = jnp.dot(xbs[r], win_ref[:, gcols], preferred_element_type=F32)
            s1, s2 = _shift_rows(a, p1, p2, row8)
            h_ref[r * rg:(r + 1) * rg, cols] = _conv_gate(a, gt, s1, s2, cw).astype(BF16)
            p1 = a[rg - 1:rg, :]
            p2 = a[rg - 2:rg - 1, :]
        carry_ref[:, cols] = a[rg - V7X_SUBLANES:, :]
        if j < len(rider_items):
            rider_items[j]()
    for item in rider_items[FFN_NCHUNK:]:
        item()
    for r in range(nsplit):
        rows = slice(r * rg, (r + 1) * rg)
        out = jnp.dot(h_ref[rows, :], wout_ref[...], preferred_element_type=F32)
        o_ref[rows, :] = _deepnorm_ln(x_ref[rows, :], out, g_ref[...], b_ref[...])

    @pl.when(i == last)
    def _():
        cs_ref[...] = carry_ref[V7X_SUBLANES - (CONV_WIDTH - 1):, :]


def _ffn(x, win, cw, wout, g, bta, rider, *, tm):
    b, t, _ = x.shape
    grid = (b, t // tm)
    tok = pl.BlockSpec((None, tm, D_MODEL), lambda bi, i: (bi, i, 0))
    cs_shape = (CONV_WIDTH - 1, D_FF)
    args = [x, win, cw, wout, g, bta]
    in_specs = [tok, _full(win.shape), _full(cw.shape), _full(wout.shape), _full(g.shape),
                _full(bta.shape)]
    out_specs = [tok, pl.BlockSpec((None,) + cs_shape, lambda bi, i: (bi, 0, 0))]
    out_shape = [jax.ShapeDtypeStruct((b, t, D_MODEL), F32), jax.ShapeDtypeStruct((b,) + cs_shape, F32)]
    aliases = {}
    plan = None
    if rider is not None:
        n_steps = grid[0] * grid[1]
        plan = rider.plan(n_steps)
        r_args, r_in, r_out_specs, r_out_shape, aliases = rider.specs(
            n_steps, lambda bi, i: bi * grid[1] + i, len(args), len(out_shape))
        args += r_args
        in_specs += r_in
        out_specs += r_out_specs
        out_shape += r_out_shape
    return pl.pallas_call(
        functools.partial(_ffn_kernel, tm=tm, nsplit=2, rider=plan),
        grid=grid,
        in_specs=in_specs,
        out_specs=out_specs,
        out_shape=out_shape,
        scratch_shapes=[
            pltpu.VMEM((tm, D_FF), BF16),
            pltpu.VMEM((V7X_SUBLANES, D_FF), F32),
        ],
        input_output_aliases=aliases,
        compiler_params=_params("arbitrary", "arbitrary"),
        name="conv_ffn",
    )(*args)


def _sample_attn_kernel(sink_ref, x_ref, wq_ref, wkv_ref, wo_ref, cos_ref, sin_ref, ck_ref, cv_ref,
                        g_ref, b_ref, *rest, bb):
    o_ref, kw_ref, vw_ref, q_scr, kv_scr, o_scr = rest[-6:]
    i = pl.program_id(0)
    last = pl.num_programs(0) - 1
    n = x_ref.shape[0]
    w = WINDOW

    @pl.when(i == 0)
    def _():
        xb = x_ref[...].astype(BF16)
        lane = lax.broadcasted_iota(jnp.int32, (n, V7X_LANES), 1)
        first_half = (lane & 63) < 32
        cos = cos_ref[...]
        sin = sin_ref[...]
        zq = jnp.dot(xb, wq_ref[...], preferred_element_type=F32)
        for c in range(NQ // 128):
            q_scr[:, c * 128:(c + 1) * 128] = _rope_128(
                zq[:, c * 128:(c + 1) * 128], cos, sin, first_half) * ATTN_SCALE
        zkv = jnp.dot(xb, wkv_ref[...], preferred_element_type=F32)
        for c in range(NKV // 128):
            kv_scr[:, c * 128:(c + 1) * 128] = _rope_128(
                zkv[:, c * 128:(c + 1) * 128], cos, sin, first_half)
        kv_scr[:, NKV:] = zkv[:, NKV:]

    r16 = lax.broadcasted_iota(jnp.int32, (ATTN_HEADS, NKV), 0)
    l16 = lax.broadcasted_iota(jnp.int32, (ATTN_HEADS, NKV), 1)
    diag = (l16 >> 6) == (r16 & (ATTN_KV_HEADS - 1))
    rowkey = lax.broadcasted_iota(jnp.int32, (w, NKV), 0)
    sink = sink_ref[...]

    def token(bi, carry):
        t = i * bb + bi
        qrow = q_scr[pl.ds(t, 1), :]
        knew = kv_scr[pl.ds(t, 1), 0:NKV]
        vnew = kv_scr[pl.ds(t, 1), NKV:]
        kc = ck_ref[bi]
        vc = cv_ref[bi]
        kw_ref[bi] = jnp.where(rowkey == w - 1, knew, pltpu.roll(kc, w - 1, 0))
        vw_ref[bi] = jnp.where(rowkey == w - 1, vnew, pltpu.roll(vc, w - 1, 0))
        qrep = jnp.broadcast_to(qrow[:, 3 * NKV:], (ATTN_HEADS, NKV))
        for hh in (2, 1, 0):
            qrep = jnp.where(r16 < (hh + 1) * ATTN_KV_HEADS,
                             jnp.broadcast_to(qrow[:, hh * NKV:(hh + 1) * NKV], (ATTN_HEADS, NKV)), qrep)
        qbd = jnp.where(diag, qrep, 0.0)
        s = lax.dot_general(qbd.astype(BF16), kc.astype(BF16), (((1,), (1,)), ((), ())),
                            preferred_element_type=F32)
        s_new = jnp.sum(qbd * knew, axis=-1, keepdims=True)
        m = jnp.maximum(jnp.maximum(jnp.max(s, axis=-1, keepdims=True), s_new), sink)
        e = jnp.exp(s - m)
        e_new = jnp.exp(s_new - m)
        denom = jnp.sum(e, axis=-1, keepdims=True) + e_new + jnp.exp(sink - m)
        o = jnp.dot(e.astype(BF16), vc.astype(BF16), preferred_element_type=F32) + e_new * vnew
        o = jnp.where(diag, o / denom, 0.0)
        for hh in range(ATTN_GROUP):
            part = jnp.where((r16 >> 2) == hh, o, 0.0)
            o_scr[pl.ds(t, 1), hh * NKV:(hh + 1) * NKV] = jnp.sum(part, axis=0, keepdims=True)
        return carry

    lax.fori_loop(0, bb, token, 0, unroll=2)

    @pl.when(i == last)
    def _():
        out = jnp.dot(o_scr[...].astype(BF16), wo_ref[...], preferred_element_type=F32)
        o_ref[...] = _deepnorm_ln(x_ref[...], out, g_ref[...], b_ref[...])


def _sample_attn(sinks_perm, x, wq_perm, wkv, wo_perm, cos, sin, cache_k_all, cache_v_all, g, bta,
                 prev_kv, *, layer, bb):
    n = x.shape[0]
    grid = (n // bb,)
    cache = pl.BlockSpec((None, bb, WINDOW, NKV), lambda i: (layer, i, 0, 0))
    args = [sinks_perm, x, wq_perm, wkv, wo_perm, cos, sin, cache_k_all, cache_v_all, g, bta]
    prev = list(prev_kv)
    prev_specs, aliases = _stacked_update(layer, prev, 1, len(args))
    return pl.pallas_call(
        functools.partial(_sample_attn_kernel, bb=bb),
        grid=grid,
        in_specs=[
            _full(sinks_perm.shape),
            _full(x.shape), _full(wq_perm.shape), _full(wkv.shape), _full(wo_perm.shape),
            _full(cos.shape), _full(sin.shape), cache, cache, _full(g.shape), _full(bta.shape),
        ] + prev_specs,
        out_specs=[_full((n, D_MODEL)), cache, cache],
        out_shape=[
            jax.ShapeDtypeStruct((n, D_MODEL), F32),
            jax.ShapeDtypeStruct(cache_k_all.shape, F32),
            jax.ShapeDtypeStruct(cache_v_all.shape, F32),
        ],
        scratch_shapes=[
            pltpu.VMEM((n, NQ), F32),
            pltpu.VMEM((n, 2 * NKV), F32),
            pltpu.VMEM((n, NQ), F32),
        ],
        input_output_aliases=aliases,
        compiler_params=_params("arbitrary"),
        name="sample_attn",
    )(*args, *prev)


def _sample_ret_in_kernel(x_ref, w_ref, cos_ref, sin_ref, qt_ref, kt_ref, v_ref, sg_ref):
    n = x_ref.shape[0]
    xb = x_ref[...].astype(BF16)
    lane = lax.broadcasted_iota(jnp.int32, (n, V7X_LANES), 1)
    even = (lane & 1) == 0
    for sec, out_ref, scale in ((0, qt_ref, None), (1, kt_ref, RET_K_SCALE)):
        z = jnp.dot(xb, w_ref[:, sec * RET_NQK:(sec + 1) * RET_NQK], preferred_element_type=F32)
        for c in range(RET_NQK // 128):
            t0 = (c % 2) * 128
            zc = _pair_rot_128(z[:, c * 128:(c + 1) * 128], cos_ref[:, t0:t0 + 128],
                               sin_ref[:, t0:t0 + 128], even)
            if scale is not None:
                zc = zc * scale
            out_ref[c * 128:(c + 1) * 128, :] = zc.T
    base = 2 * RET_NQK
    v_ref[...] = jnp.dot(xb, w_ref[:, base:base + RET_NV], preferred_element_type=F32)
    base = 2 * RET_NQK + RET_NV
    sg_ref[...] = _silu(jnp.dot(xb, w_ref[:, base:base + RET_NV], preferred_element_type=F32))


def _sample_ret_in(x, w_in_bf16, cos, sin):
    n = x.shape[0]
    return pl.pallas_call(
        _sample_ret_in_kernel,
        grid=(1,),
        in_specs=[_full(x.shape), _full(w_in_bf16.shape), _full(cos.shape), _full(sin.shape)],
        out_specs=[_full((RET_NQK, n)), _full((RET_NQK, n)), _full((n, RET_NV)), _full((n, RET_NV))],
        out_shape=[
            jax.ShapeDtypeStruct((RET_NQK, n), F32),
            jax.ShapeDtypeStruct((RET_NQK, n), F32),
            jax.ShapeDtypeStruct((n, RET_NV), F32),
            jax.ShapeDtypeStruct((n, RET_NV), F32),
        ],
        compiler_params=_params("arbitrary"),
        name="sample_ret_in",
    )(x, w_in_bf16, cos, sin)


def _stacked_update(layer, prev_outs, first_out_index, n_inputs):
    specs = [pl.BlockSpec(memory_space=pl.ANY) for _ in prev_outs]
    aliases = {n_inputs + k: first_out_index + k for k in range(len(prev_outs))}
    return specs, aliases


def _sample_ret_out_kernel(o_ref, sg_ref, x_ref, wo_ref, g_ref, b_ref, y_ref):
    n = x_ref.shape[0]
    acc = jnp.zeros((n, D_MODEL), F32)
    for h in range(RET_HEADS):
        vcols = slice(h * RET_VALUE_DIM, (h + 1) * RET_VALUE_DIM)
        o = o_ref[:, vcols]
        mu = jnp.mean(o, axis=-1, keepdims=True)
        d = o - mu
        var = jnp.mean(d * d, axis=-1, keepdims=True)
        gated = (sg_ref[:, vcols] * (d * lax.rsqrt(var + GN_EPS))).astype(BF16)
        acc = acc + jnp.dot(gated, wo_ref[vcols, :], preferred_element_type=F32)
    y_ref[...] = _deepnorm_ln(x_ref[...], acc, g_ref[...], b_ref[...])


def _sample_ret_out(o, sg, x, wo_bf16, g, bta):
    args = (o, sg, x, wo_bf16, g, bta)
    return pl.pallas_call(
        _sample_ret_out_kernel,
        grid=(1,),
        in_specs=[_full(a.shape) for a in args],
        out_specs=_full(x.shape),
        out_shape=jax.ShapeDtypeStruct(x.shape, F32),
        compiler_params=_params("arbitrary"),
        name="sample_ret_out",
    )(*args)


def _sample_ffn_kernel(x_ref, b0_ref, b1_ref, win_ref, cw_ref, wout_ref, g_ref, b_ref, y_ref, a_ref):
    n = x_ref.shape[0]
    xb = x_ref[...].astype(BF16)
    acc = jnp.zeros((n, D_MODEL), F32)
    for j in range(FFN_NCHUNK):
        cols = slice(j * FFN_CHUNK, (j + 1) * FFN_CHUNK)
        gcols = slice(D_FF + j * FFN_CHUNK, D_FF + (j + 1) * FFN_CHUNK)
        a = jnp.dot(xb, win_ref[:, cols], preferred_element_type=F32)
        gt = jnp.dot(xb, win_ref[:, gcols], preferred_element_type=F32)
        h = _conv_gate(a, gt, b1_ref[:, cols], b0_ref[:, cols], cw_ref[:, cols])
        a_ref[:, cols] = a
        acc = acc + jnp.dot(h.astype(BF16), wout_ref[cols, :], preferred_element_type=F32)
    y_ref[...] = _deepnorm_ln(x_ref[...], acc, g_ref[...], b_ref[...])


def _sample_ffn(x, buf0, buf1, win, cw, wout, g, bta):
    args = (x, buf0, buf1, win, cw, wout, g, bta)
    return pl.pallas_call(
        _sample_ffn_kernel,
        grid=(1,),
        in_specs=[_full(a.shape) for a in args],
        out_specs=[_full(x.shape), _full(buf0.shape)],
        out_shape=[jax.ShapeDtypeStruct(x.shape, F32), jax.ShapeDtypeStruct(buf0.shape, F32)],
        compiler_params=_params("arbitrary"),
        name="sample_ffn",
    )(*args)


def _head_major_to_group_minor(w, axis):
    shape = w.shape
    split = shape[:axis] + (ATTN_KV_HEADS, ATTN_GROUP, ATTN_HEAD_DIM) + shape[axis + 1:]
    return jnp.swapaxes(w.reshape(split), axis, axis + 1).reshape(shape)


def kernel(x_prompt, x_sample, cache_k_win, cache_v_win, state_ret, state_conv, attn_w_qkv, attn_sinks, attn_w_o, ret_w_in, ret_w_o, ffn_w_in, ffn_conv_w, ffn_conv_b, ffn_w_out, ln_mix_g, ln_mix_b, ln_ffn_g, ln_ffn_b):
    bp, tp, _ = x_prompt.shape
    ns = x_sample.shape[0]
    assert x_sample.shape[1] == 1, "the sample group carries one new token per sequence"
    xp = x_prompt
    xs = x_sample.reshape(ns, D_MODEL)
    pos_p = jnp.arange(tp)
    pos_s = jnp.full((ns,), PAST_LEN, jnp.int32)
    rope_p = _rope_tables(pos_p)
    rope_s = _rope_tables(pos_s)
    rot_p = _ret_rot_tables(pos_p)
    rot_s = _ret_rot_tables(pos_s)
    decay_in, q_decay, k_decay, chunk_decay = _ret_decay_tables(RET_CHUNK)
    gamma = jnp.exp(_ret_log_gamma())
    row = lambda v: v.reshape(1, D_MODEL)

    tm = min(512, tp)
    cache_k_all = cache_k_win.reshape(cache_k_win.shape[0], ns, WINDOW, NKV)
    cache_v_all = cache_v_win.reshape(cache_v_win.shape[0], ns, WINDOW, NKV)
    kv_s = ()
    rs = None
    kp_l, vp_l, rp_l, cp_l, cs_l = [], [], [], [], []
    for i in range(DEPTH):
        j = i // N_MIXERS
        g_mix, b_mix = row(ln_mix_g[i]), row(ln_mix_b[i])
        if i % N_MIXERS == 0:
            w_qkv = attn_w_qkv[j].astype(BF16)
            w_o = attn_w_o[j].astype(BF16)
            q, k2, v2, kf, vf = _attn_qkv(xp, w_qkv, *rope_p, tm=tm)
            xp = _attn_core(attn_sinks[j], q, k2, v2, xp, w_o, g_mix, b_mix, tq=tm)
            kp_l.append(kf.reshape(bp, WINDOW, ATTN_KV_HEADS, ATTN_HEAD_DIM))
            vp_l.append(vf.reshape(bp, WINDOW, ATTN_KV_HEADS, ATTN_HEAD_DIM))
            xs, *kv_s = _sample_attn(
                attn_sinks[j].reshape(ATTN_KV_HEADS, ATTN_GROUP).T.reshape(ATTN_HEADS, 1), xs,
                _head_major_to_group_minor(w_qkv[:, :NQ], 1), w_qkv[:, NQ:],
                _head_major_to_group_minor(w_o, 0), *rope_s, cache_k_all, cache_v_all,
                g_mix, b_mix, kv_s, layer=j, bb=min(16, ns))
        else:
            w_in = ret_w_in[j].astype(BF16)
            w_o = ret_w_o[j].astype(BF16)
            q, k, v, sg = _ret_in(xp, w_in, *rot_p, tm=tm)
            qt, kt, vs, sgs = _sample_ret_in(xs, w_in, *rot_s)
            half = ns // 2
            rider = _RetStateRider(gamma, qt, kt, vs, state_ret, rs, j, 0, half)
            xp, rp, os_a, rs = _ret_core(chunk_decay, q, k, v, sg, xp, decay_in, q_decay, k_decay,
                                         w_o, g_mix, b_mix, rider, ct=tm)
            rp_l.append(rp)
            ffn_rider = _RetStateRider(gamma, qt, kt, vs, state_ret, rs, j, half, ns - half)
        w_in = ffn_w_in[i].astype(BF16)
        wout = ffn_w_out[i].astype(BF16)
        cw = jnp.concatenate(
            [ffn_conv_w[i], ffn_conv_b[i][None], jnp.zeros((V7X_SUBLANES - CONV_WIDTH - 1, D_FF), F32)], axis=0)
        g_ffn, b_ffn = row(ln_ffn_g[i]), row(ln_ffn_b[i])
        if i % N_MIXERS == 0:
            xp, cp = _ffn(xp, w_in, cw, wout, g_ffn, b_ffn, None, tm=tm)
        else:
            xp, cp, os_b, rs = _ffn(xp, w_in, cw, wout, g_ffn, b_ffn, ffn_rider, tm=tm)
            xs = _sample_ret_out(jnp.concatenate([os_a, os_b], axis=0), sgs, xs, w_o, g_mix, b_mix)
        cp_l.append(cp)
        buf = state_conv[i]
        xs, a_new = _sample_ffn(xs, buf[:, 0, :], buf[:, 1, :], w_in, cw, wout, g_ffn, b_ffn)
        cs_l.append(jnp.stack([buf[:, 1, :], a_new], axis=1))
    ks, vs = (w.reshape(cache_k_win.shape) for w in kv_s)
    return (xp, xs.reshape(ns, 1, D_MODEL),
            jnp.stack(kp_l), jnp.stack(vp_l), jnp.stack(rp_l), jnp.stack(cp_l),
            ks, vs, rs, jnp.stack(cs_l))
```

```python
import functools
from typing import NamedTuple, Optional

import jax
import jax.numpy as jnp
from jax import lax
from jax.experimental import pallas as pl
from jax.experimental.pallas import tpu as pltpu

F32 = jnp.float32
BF16 = jnp.bfloat16

D_MODEL = 1024
DEPTH = 4
PAST_LEN = 8192
N_MIXERS = 2
ATTN_HEAD_DIM = 64
ATTN_HEADS = 16
ATTN_KV_HEADS = 4
ATTN_GROUP = ATTN_HEADS // ATTN_KV_HEADS
WINDOW = 128
ROPE_THETA = 10000.0
RET_KEY_DIM = 256
RET_HEADS = 4
RET_VALUE_DIM = 512
RET_CHUNK = 128
D_FF = 2816
CONV_WIDTH = 3
LN_EPS = 1e-5
GN_EPS = 1e-5
DEEPNORM_ALPHA = (2.0 * DEPTH) ** 0.25
ATTN_SCALE = ATTN_HEAD_DIM ** -0.5
RET_K_SCALE = RET_KEY_DIM ** -0.5

V7X_LANES = 128
V7X_SUBLANES = 8
V7X_VMEM_LIMIT_BYTES = 56 * 1024 * 1024

NQ = ATTN_HEADS * ATTN_HEAD_DIM
NKV = ATTN_KV_HEADS * ATTN_HEAD_DIM
RET_NQK = RET_HEADS * RET_KEY_DIM
RET_NV = RET_HEADS * RET_VALUE_DIM
FFN_CHUNK = 256
FFN_NCHUNK = D_FF // FFN_CHUNK
MASKED_SCORE = -1e30


def _params(*semantics, flags=None):
    return pltpu.CompilerParams(
        dimension_semantics=semantics, vmem_limit_bytes=V7X_VMEM_LIMIT_BYTES, flags=flags
    )


def _full(shape):
    n = len(shape)
    return pl.BlockSpec(shape, lambda *_: (0,) * n)


class _LayerOf(NamedTuple):
    stacked: jax.Array
    layer: int


def _spec(a):
    if not isinstance(a, _LayerOf):
        return _full(a.shape)
    shape = a.stacked.shape[1:]
    return pl.BlockSpec((None,) + shape, lambda *_: (a.layer,) + (0,) * len(shape),
                        pipeline_mode=pl.Buffered(1))


def _arr(a):
    return a.stacked if isinstance(a, _LayerOf) else a


def _deepnorm_ln(x, m, g, b):
    y = DEEPNORM_ALPHA * x + m
    mu = jnp.mean(y, axis=-1, keepdims=True)
    d = y - mu
    var = jnp.mean(d * d, axis=-1, keepdims=True)
    return d * lax.rsqrt(var + LN_EPS) * g + b


def _silu(z):
    return z * jax.nn.sigmoid(z)


def _rope_tables(pos):
    half = ATTN_HEAD_DIM // 2
    lane = jnp.arange(V7X_LANES)
    inv_freq = ROPE_THETA ** (-(lane % half).astype(F32) / half)
    sign = jnp.where((lane % ATTN_HEAD_DIM) < half, -1.0, 1.0).astype(F32)
    ang = pos.astype(F32)[:, None] * inv_freq[None, :]
    return jnp.cos(ang), jnp.sin(ang) * sign[None, :]


def _ret_rot_tables(pos):
    d = RET_KEY_DIM
    angle = jnp.repeat(1.0 / (10000.0 ** jnp.linspace(0.0, 1.0, d // 2, dtype=F32)), 2)
    sign = jnp.where(jnp.arange(d) % 2 == 0, -1.0, 1.0).astype(F32)
    ang = pos.astype(F32)[:, None] * angle[None, :]
    return jnp.cos(ang), jnp.sin(ang) * sign[None, :]


def _ret_log_gamma():
    return jnp.log(1.0 - 2.0 ** (-5.0 - jnp.arange(RET_HEADS, dtype=F32)))


def _ret_decay_tables(chunk):
    log_gamma = _ret_log_gamma()
    idx = jnp.arange(chunk, dtype=F32)
    rel = idx[:, None] - idx[None, :]
    decay_in = jnp.where(
        rel[None] >= 0,
        jnp.exp(log_gamma[:, None, None] * jnp.maximum(rel, 0.0)[None]),
        0.0,
    )
    q_decay = jnp.exp(log_gamma[:, None] * (idx[None, :] + 1.0))
    k_decay = jnp.exp(log_gamma[:, None] * (chunk - 1.0 - idx[None, :]))
    chunk_decay = jnp.exp(log_gamma * chunk)
    lane_bcast = lambda t: jnp.broadcast_to(t[:, :, None], (RET_HEADS, chunk, V7X_LANES))
    return decay_in, lane_bcast(q_decay), lane_bcast(k_decay), chunk_decay


def _rope_128(z, cos, sin_signed, first_half):
    partner = jnp.where(first_half, pltpu.roll(z, 96, 1), pltpu.roll(z, 32, 1))
    return z * cos + partner * sin_signed


def _pair_rot_128(z, cos, sin_signed, even_lane):
    partner = jnp.where(even_lane, pltpu.roll(z, 127, 1), pltpu.roll(z, 1, 1))
    return z * cos + partner * sin_signed


def _attn_qkv_kernel(x_ref, w_ref, cos_ref, sin_ref, q_ref, k2_ref, v2_ref, kf_ref, vf_ref, *, tm):
    i = pl.program_id(1)
    last = pl.num_programs(1) - 1
    xb = x_ref[...].astype(BF16)
    cos = cos_ref[...]
    sin = sin_ref[...]
    lane = lax.broadcasted_iota(jnp.int32, (tm, V7X_LANES), 1)
    first_half = (lane & 63) < 32
    lo = lane < 64

    def dup(z, take_low):
        zr = pltpu.roll(z, 64, 1)
        return jnp.where(lo, z, zr) if take_low else jnp.where(lo, zr, z)

    for j in range(NQ // 512):
        z = jnp.dot(xb, w_ref[:, j * 512:(j + 1) * 512], preferred_element_type=F32)
        for c in range(4):
            zc = _rope_128(z[:, c * 128:(c + 1) * 128], cos, sin, first_half) * ATTN_SCALE
            q_ref[:, j * 512 + c * 128:j * 512 + (c + 1) * 128] = zc.astype(BF16)
    zkv = jnp.dot(xb, w_ref[:, NQ:NQ + 2 * NKV], preferred_element_type=F32)
    for c in range(NKV // 128):
        kc = _rope_128(zkv[:, c * 128:(c + 1) * 128], cos, sin, first_half)
        vc = zkv[:, NKV + c * 128:NKV + (c + 1) * 128]
        for gg in range(2):
            g = 2 * c + gg
            k2_ref[:, g * 128:(g + 1) * 128] = dup(kc, gg == 0).astype(BF16)
            v2_ref[:, g * 128:(g + 1) * 128] = dup(vc, gg == 0).astype(BF16)

        @pl.when(i == last)
        def _():
            kf_ref[:, c * 128:(c + 1) * 128] = kc[tm - WINDOW:, :]
            vf_ref[:, c * 128:(c + 1) * 128] = vc[tm - WINDOW:, :]


def _attn_qkv(x, w_qkv_bf16, cos, sin, *, tm):
    b, t, _ = x.shape
    grid = (b, t // tm)
    tok = lambda w: pl.BlockSpec((None, tm, w), lambda bi, i: (bi, i, 0))
    tab = pl.BlockSpec((tm, V7X_LANES), lambda bi, i: (i, 0))
    tail = pl.BlockSpec((None, WINDOW, NKV), lambda bi, i: (bi, 0, 0))
    return pl.pallas_call(
        functools.partial(_attn_qkv_kernel, tm=tm),
        grid=grid,
        in_specs=[tok(D_MODEL), _spec(w_qkv_bf16), tab, tab],
        out_specs=[tok(NQ), tok(2 * NKV), tok(2 * NKV), tail, tail],
        out_shape=[
            jax.ShapeDtypeStruct((b, t, NQ), BF16),
            jax.ShapeDtypeStruct((b, t, 2 * NKV), BF16),
            jax.ShapeDtypeStruct((b, t, 2 * NKV), BF16),
            jax.ShapeDtypeStruct((b, WINDOW, NKV), F32),
            jax.ShapeDtypeStruct((b, WINDOW, NKV), F32),
        ],
        compiler_params=_params("arbitrary", "arbitrary"),
        name="attn_qkv",
    )(x, _arr(w_qkv_bf16), cos, sin)


def _attn_core_kernel(sink_ref, q_ref, kc_ref, kp_ref, vc_ref, vp_ref, x_ref, wo_ref, g_ref, b_ref,
                      o_ref, kcat, vcat, oscr, *, tq):
    i = pl.program_id(1)
    kcat[0:WINDOW, :] = kp_ref[...]
    kcat[WINDOW:, :] = kc_ref[...]
    vcat[0:WINDOW, :] = vp_ref[...]
    vcat[WINDOW:, :] = vc_ref[...]
    w = WINDOW
    ii = lax.broadcasted_iota(jnp.int32, (w, 2 * w), 0)
    jj = lax.broadcasted_iota(jnp.int32, (w, 2 * w), 1)
    band = (jj >= ii) & (jj <= ii + w)
    first_key = jnp.where(i == 0, w, 0)
    lo = lax.broadcasted_iota(jnp.int32, (w, V7X_LANES), 1) < 64
    zero = jnp.zeros((w, V7X_LANES), BF16)
    npair = ATTN_HEADS // 2
    blocks = [(qb, p) for qb in range(tq // w) for p in range(npair)]

    def scores(qb, p):
        g = p // (ATTN_GROUP // 2)
        q2 = q_ref[qb * w:(qb + 1) * w, p * 128:(p + 1) * 128]
        qs = jnp.concatenate([jnp.where(lo, q2, zero), jnp.where(lo, zero, q2)], axis=0)
        return lax.dot_general(qs, kcat[qb * w:(qb + 2) * w, g * 128:(g + 1) * 128],
                               (((1,), (1,)), ((), ())), preferred_element_type=F32)

    s_next = scores(*blocks[0])
    for n, (qb, p) in enumerate(blocks):
        s = s_next
        if n + 1 < len(blocks):
            s_next = scores(*blocks[n + 1])
        mask = band & (jj >= first_key) if qb == 0 else band
        g = p // (ATTN_GROUP // 2)
        es, denoms = [], []
        for r in range(2):
            sr = jnp.where(mask, s[r * w:(r + 1) * w], MASKED_SCORE)
            sink = sink_ref[2 * p + r]
            m = jnp.maximum(jnp.max(sr, axis=-1, keepdims=True), sink)
            e = jnp.exp(sr - m)
            denoms.append(jnp.sum(e, axis=-1, keepdims=True) + jnp.exp(sink - m))
            es.append(e.astype(BF16))
        o2 = jnp.dot(jnp.concatenate(es, axis=0), vcat[qb * w:(qb + 2) * w, g * 128:(g + 1) * 128],
                     preferred_element_type=F32)
        oscr[qb * w:(qb + 1) * w, p * 128:(p + 1) * 128] = jnp.where(
            lo, o2[:w] / denoms[0], o2[w:] / denoms[1]).astype(BF16)
    out = jnp.dot(oscr[...], wo_ref[...], preferred_element_type=F32)
    o_ref[...] = _deepnorm_ln(x_ref[...], out, g_ref[...], b_ref[...])


def _attn_core(sinks, q, k2, v2, x, wo_bf16, g, bta, *, tq):
    b, t, _ = x.shape
    grid = (b, t // tq)
    r = tq // WINDOW
    cur = lambda w: pl.BlockSpec((None, tq, w), lambda bi, i: (bi, i, 0))
    prev = pl.BlockSpec((None, WINDOW, 2 * NKV), lambda bi, i: (bi, jnp.maximum(i * r - 1, 0), 0))
    return pl.pallas_call(
        functools.partial(_attn_core_kernel, tq=tq),
        grid=grid,
        in_specs=[
            pl.BlockSpec(memory_space=pltpu.SMEM),
            cur(NQ), cur(2 * NKV), prev, cur(2 * NKV), prev, cur(D_MODEL),
            _spec(wo_bf16), _full(g.shape), _full(bta.shape),
        ],
        out_specs=cur(D_MODEL),
        out_shape=jax.ShapeDtypeStruct((b, t, D_MODEL), F32),
        scratch_shapes=[
            pltpu.VMEM((tq + WINDOW, 2 * NKV), BF16),
            pltpu.VMEM((tq + WINDOW, 2 * NKV), BF16),
            pltpu.VMEM((tq, NQ), BF16),
        ],
        compiler_params=_params("arbitrary", "arbitrary"),
        name="attn_core",
    )(sinks, q, k2, k2, v2, v2, x, _arr(wo_bf16), g, bta)


def _ret_in_kernel(x_ref, w_ref, cos_ref, sin_ref, q_ref, k_ref, v_ref, sg_ref, *, tm):
    xb = x_ref[...].astype(BF16)
    lane = lax.broadcasted_iota(jnp.int32, (tm, V7X_LANES), 1)
    even = (lane & 1) == 0
    nc = 512
    for sec, out_ref, scale in ((0, q_ref, None), (1, k_ref, RET_K_SCALE)):
        for j in range(RET_NQK // nc):
            z = jnp.dot(xb, w_ref[:, sec * RET_NQK + j * nc:sec * RET_NQK + (j + 1) * nc],
                        preferred_element_type=F32)
            for c in range(nc // 128):
                t0 = (c % 2) * 128
                zc = _pair_rot_128(z[:, c * 128:(c + 1) * 128], cos_ref[:, t0:t0 + 128],
                                   sin_ref[:, t0:t0 + 128], even)
                if scale is not None:
                    zc = zc * scale
                out_ref[:, j * nc + c * 128:j * nc + (c + 1) * 128] = zc.astype(BF16)
    base = 2 * RET_NQK
    for j in range(RET_NV // nc):
        z = jnp.dot(xb, w_ref[:, base + j * nc:base + (j + 1) * nc], preferred_element_type=F32)
        v_ref[:, j * nc:(j + 1) * nc] = z.astype(BF16)
    base = 2 * RET_NQK + RET_NV
    for j in range(RET_NV // nc):
        z = jnp.dot(xb, w_ref[:, base + j * nc:base + (j + 1) * nc], preferred_element_type=F32)
        sg_ref[:, j * nc:(j + 1) * nc] = _silu(z).astype(BF16)


def _ret_in(x, w_in_bf16, cos, sin, *, tm):
    b, t, _ = x.shape
    grid = (b, t // tm)
    tok = lambda w: pl.BlockSpec((None, tm, w), lambda bi, i: (bi, i, 0))
    tab = pl.BlockSpec((tm, RET_KEY_DIM), lambda bi, i: (i, 0))
    return pl.pallas_call(
        functools.partial(_ret_in_kernel, tm=tm),
        grid=grid,
        in_specs=[tok(D_MODEL), _spec(w_in_bf16), tab, tab],
        out_specs=[tok(RET_NQK), tok(RET_NQK), tok(RET_NV), tok(RET_NV)],
        out_shape=[
            jax.ShapeDtypeStruct((b, t, RET_NQK), BF16),
            jax.ShapeDtypeStruct((b, t, RET_NQK), BF16),
            jax.ShapeDtypeStruct((b, t, RET_NV), BF16),
            jax.ShapeDtypeStruct((b, t, RET_NV), BF16),
        ],
        compiler_params=_params("arbitrary", "arbitrary"),
        name="ret_in",
    )(x, _arr(w_in_bf16), cos, sin)


def _ret_core_kernel(cd_ref, q_ref, k_ref, v_ref, sg_ref, x_ref, din_ref, qd_ref, kd_ref, wo_ref,
                     g_ref, b_ref, *rest, ct, rider):
    nr = rider.n_in if rider else 0
    o_ref, st_ref = rest[nr:nr + 2]
    s_scr, gated_scr = rest[-2:]
    i = pl.program_id(1)
    last = pl.num_programs(1) - 1
    c = RET_CHUNK
    nchunk = ct // c
    step = pl.program_id(0) * pl.num_programs(1) + i
    rider_items = rider.items(rest[:nr], rest[nr + 2:-2], step) if rider else []
    per_chunk = -(-len(rider_items) // nchunk)

    @pl.when(i == 0)
    def _():
        s_scr[...] = jnp.zeros_like(s_scr)

    heads = range(RET_HEADS)
    kcols = [slice(h * RET_KEY_DIM, (h + 1) * RET_KEY_DIM) for h in heads]
    vcols = [slice(h * RET_VALUE_DIM, (h + 1) * RET_VALUE_DIM) for h in heads]
    nt = (((1,), (1,)), ((), ()))
    tn = (((0,), (0,)), ((), ()))
    for ci in range(ct // c):
        rows = slice(ci * c, (ci + 1) * c)
        qs = [q_ref[rows, kcols[h]] for h in heads]
        ks = [k_ref[rows, kcols[h]] for h in heads]
        vs = [v_ref[rows, vcols[h]] for h in heads]
        inner = [lax.dot_general(qs[h], ks[h], nt, preferred_element_type=F32) for h in heads]
        inner = [(inner[h] * din_ref[h]).astype(BF16) for h in heads]
        o = []
        for h in heads:
            qd = jnp.concatenate([qd_ref[h]] * (RET_KEY_DIM // V7X_LANES), axis=1)
            qdec = (qs[h].astype(F32) * qd).astype(BF16)
            o.append(jnp.dot(inner[h], vs[h], preferred_element_type=F32)
                     + jnp.dot(qdec, s_scr[h].astype(BF16), preferred_element_type=F32))
        for h in heads:
            mu = jnp.mean(o[h], axis=-1, keepdims=True)
            d = o[h] - mu
            var = jnp.mean(d * d, axis=-1, keepdims=True)
            on = d * lax.rsqrt(var + GN_EPS)
            gated_scr[rows, vcols[h]] = (sg_ref[rows, vcols[h]].astype(F32) * on).astype(BF16)
        for h in heads:
            kd = jnp.concatenate([kd_ref[h]] * (RET_KEY_DIM // V7X_LANES), axis=1)
            kdec = (ks[h].astype(F32) * kd).astype(BF16)
            s_scr[h] = s_scr[h] * cd_ref[h] + lax.dot_general(
                kdec, vs[h], tn, preferred_element_type=F32)
        for item in rider_items[ci * per_chunk:(ci + 1) * per_chunk]:
            item()
        out = jnp.dot(gated_scr[rows, :], wo_ref[...], preferred_element_type=F32)
        o_ref[rows, :] = _deepnorm_ln(x_ref[rows, :], out, g_ref[...], b_ref[...])

    @pl.when(i == last)
    def _():
        st_ref[...] = s_scr[...]


def _ret_core(chunk_decay, q, k, v, sg, x, decay_in, q_decay, k_decay, wo_bf16, g, bta, rider, *, ct):
    b, t, _ = x.shape
    grid = (b, t // ct)
    tok = lambda w: pl.BlockSpec((None, ct, w), lambda bi, i: (bi, i, 0))
    st_shape = (RET_HEADS, RET_KEY_DIM, RET_VALUE_DIM)
    args = [chunk_decay, q, k, v, sg, x, decay_in, q_decay, k_decay, _arr(wo_bf16), g, bta]
    in_specs = [
        pl.BlockSpec(memory_space=pltpu.SMEM),
        tok(RET_NQK), tok(RET_NQK), tok(RET_NV), tok(RET_NV), tok(D_MODEL),
        _full(decay_in.shape), _full(q_decay.shape), _full(k_decay.shape),
        _spec(wo_bf16), _full(g.shape), _full(bta.shape),
    ]
    out_specs = [tok(D_MODEL), pl.BlockSpec((None,) + st_shape, lambda bi, i: (bi, 0, 0, 0))]
    out_shape = [jax.ShapeDtypeStruct((b, t, D_MODEL), F32), jax.ShapeDtypeStruct((b,) + st_shape, F32)]
    aliases = {}
    plan = None
    if rider is not None:
        n_steps = grid[0] * grid[1]
        plan = rider.plan(n_steps)
        r_args, r_in, r_out_specs, r_out_shape, aliases = rider.specs(
            n_steps, lambda bi, i: bi * grid[1] + i, len(args), len(out_shape))
        args += r_args
        in_specs += r_in
        out_specs += r_out_specs
        out_shape += r_out_shape
    return pl.pallas_call(
        functools.partial(_ret_core_kernel, ct=ct, rider=plan),
        grid=grid,
        in_specs=in_specs,
        out_specs=out_specs,
        out_shape=out_shape,
        scratch_shapes=[pltpu.VMEM(st_shape, F32), pltpu.VMEM((ct, RET_NV), BF16)],
        input_output_aliases=aliases,
        compiler_params=_params("arbitrary", "arbitrary"),
        name="ret_core",
    )(*args)


def _ret_state_item(gam_ref, qt_ref, kt_ref, v_ref, s_ref, o_ref, sn_ref, onehot, t, row, bi, h):
    krows = slice(h * RET_KEY_DIM, (h + 1) * RET_KEY_DIM)
    vcols = slice(h * RET_VALUE_DIM, (h + 1) * RET_VALUE_DIM)
    qc = jnp.sum(qt_ref[krows, :] * onehot, axis=1, keepdims=True)
    kc = jnp.sum(kt_ref[krows, :] * onehot, axis=1, keepdims=True)
    vrow = v_ref[pl.ds(t, 1), vcols]
    s = s_ref[bi, h]
    gamma = gam_ref[h]
    cross = jnp.sum(s * qc, axis=0, keepdims=True)
    qk = jnp.sum(qc * kc, axis=0, keepdims=True)
    o_ref[pl.ds(row, 1), vcols] = qk * vrow + cross * gamma
    sn_ref[bi, h] = s * gamma + kc * vrow


class _RiderPlan(NamedTuple):
    n_in: int
    first: int
    per_step: int

    def items(self, in_refs, out_refs, step):
        gam_ref, qt_ref, kt_ref, v_ref, s_ref = in_refs[:5]
        o_ref, sn_ref = out_refs
        lane = lax.broadcasted_iota(jnp.int32, (1, qt_ref.shape[1]), 1)
        items = []
        for bi in range(self.per_step):
            row = step * self.per_step + bi
            t = self.first + row
            onehot = (lane == t).astype(F32)
            for h in range(RET_HEADS):
                items.append(functools.partial(
                    _ret_state_item, gam_ref, qt_ref, kt_ref, v_ref, s_ref, o_ref, sn_ref,
                    onehot, t, row, bi, h))
        return items


class _RetStateRider(NamedTuple):
    gamma: jax.Array
    qt: jax.Array
    kt: jax.Array
    v: jax.Array
    state_all: jax.Array
    prev_out: Optional[jax.Array]
    layer: int
    first: int
    count: int

    def plan(self, n_steps):
        return _RiderPlan(5 + (self.prev_out is not None), self.first, self.count // n_steps)

    def specs(self, n_steps, flat_index, n_inputs, n_outputs):
        per_step = self.count // n_steps
        blk0 = self.first // per_step
        st = pl.BlockSpec((None, per_step, RET_HEADS, RET_KEY_DIM, RET_VALUE_DIM),
                          lambda *g: (self.layer, blk0 + flat_index(*g), 0, 0, 0))
        args = [self.gamma, self.qt, self.kt, self.v, self.state_all]
        in_specs = [pl.BlockSpec(memory_space=pltpu.SMEM), _full(self.qt.shape), _full(self.kt.shape),
                    _full(self.v.shape), st]
        aliases = {}
        if self.prev_out is not None:
            args.append(self.prev_out)
            in_specs.append(pl.BlockSpec(memory_space=pl.ANY))
            aliases = {n_inputs + 5: n_outputs + 1}
        out_specs = [_full((self.count, RET_NV)), st]
        out_shape = [jax.ShapeDtypeStruct((self.count, RET_NV), F32),
                     jax.ShapeDtypeStruct(self.state_all.shape, F32)]
        return args, in_specs, out_specs, out_shape, aliases


def _conv_gate(a, gt, s1, s2, cw):
    c = cw[3:4, :] + s2 * cw[0:1, :]
    c = c + s1 * cw[1:2, :]
    c = c + a * cw[2:3, :]
    return _silu(c) * gt


def _shift_rows(a, p1, p2, row8):
    r1 = pltpu.roll(a, 1, 0)
    r2 = pltpu.roll(a, 2, 0)
    s = V7X_SUBLANES
    head1 = jnp.where(row8 == 0, p1, r1[:s])
    head2 = jnp.where(row8 == 0, p2, jnp.where(row8 == 1, p1, r2[:s]))
    return (jnp.concatenate([head1, r1[s:]], axis=0), jnp.concatenate([head2, r2[s:]], axis=0))


def _ffn_kernel(x_ref, win_ref, cw_ref, wout_ref, g_ref, b_ref, *rest, tm, nsplit, nsplit_out, rider):
    nr = rider.n_in if rider else 0
    o_ref, cs_ref = rest[nr:nr + 2]
    h_ref, carry_ref = rest[-2:]
    i = pl.program_id(1)
    last = pl.num_programs(1) - 1
    step = pl.program_id(0) * pl.num_programs(1) + i

    @pl.when(i == 0)
    def _():
        carry_ref[...] = jnp.zeros_like(carry_ref)

    rider_items = rider.items(rest[:nr], rest[nr + 2:-2], step) if rider else []
    rg = tm // nsplit
    row8 = lax.broadcasted_iota(jnp.int32, (V7X_SUBLANES, FFN_CHUNK), 0)
    xbs = [x_ref[r * rg:(r + 1) * rg, :].astype(BF16) for r in range(nsplit)]
    for j in range(FFN_NCHUNK):
        cols = slice(j * FFN_CHUNK, (j + 1) * FFN_CHUNK)
        gcols = slice(D_FF + j * FFN_CHUNK, D_FF + (j + 1) * FFN_CHUNK)
        cw = cw_ref[:, cols]
        prev = carry_ref[:, cols]
        p1 = prev[7:8, :]
        p2 = prev[6:7, :]
        for r in range(nsplit):
            a = jnp.dot(xbs[r], win_ref[:, cols], preferred_element_type=F32)
            gt = jnp.dot(xbs[r], win_ref[:, gcols], preferred_element_type=F32)
            s1, s2 = _shift_rows(a, p1, p2, row8)
            h_ref[r * rg:(r + 1) * rg, cols] = _conv_gate(a, gt, s1, s2, cw).astype(BF16)
            p1 = a[rg - 1:rg, :]
            p2 = a[rg - 2:rg - 1, :]
        carry_ref[:, cols] = a[rg - V7X_SUBLANES:, :]
    og = tm // nsplit_out
    per_group = -(-len(rider_items) // nsplit_out)
    for r in range(nsplit_out):
        rows = slice(r * og, (r + 1) * og)
        out = jnp.dot(h_ref[rows, :], wout_ref[...], preferred_element_type=F32)
        for item in rider_items[r * per_group:(r + 1) * per_group]:
            item()
        o_ref[rows, :] = _deepnorm_ln(x_ref[rows, :], out, g_ref[...], b_ref[...])

    @pl.when(i == last)
    def _():
        cs_ref[...] = carry_ref[V7X_SUBLANES - (CONV_WIDTH - 1):, :]


def _ffn(x, win, cw, wout, g, bta, rider, *, tm):
    b, t, _ = x.shape
    grid = (b, t // tm)
    tok = pl.BlockSpec((None, tm, D_MODEL), lambda bi, i: (bi, i, 0))
    cs_shape = (CONV_WIDTH - 1, D_FF)
    args = [x, _arr(win), cw, _arr(wout), g, bta]
    in_specs = [tok, _spec(win), _full(cw.shape), _spec(wout), _full(g.shape),
                _full(bta.shape)]
    out_specs = [tok, pl.BlockSpec((None,) + cs_shape, lambda bi, i: (bi, 0, 0))]
    out_shape = [jax.ShapeDtypeStruct((b, t, D_MODEL), F32), jax.ShapeDtypeStruct((b,) + cs_shape, F32)]
    aliases = {}
    plan = None
    if rider is not None:
        n_steps = grid[0] * grid[1]
        plan = rider.plan(n_steps)
        r_args, r_in, r_out_specs, r_out_shape, aliases = rider.specs(
            n_steps, lambda bi, i: bi * grid[1] + i, len(args), len(out_shape))
        args += r_args
        in_specs += r_in
        out_specs += r_out_specs
        out_shape += r_out_shape
    return pl.pallas_call(
        functools.partial(_ffn_kernel, tm=tm, nsplit=2, nsplit_out=2, rider=plan),
        grid=grid,
        in_specs=in_specs,
        out_specs=out_specs,
        out_shape=out_shape,
        scratch_shapes=[
            pltpu.VMEM((tm, D_FF), BF16),
            pltpu.VMEM((V7X_SUBLANES, D_FF), F32),
        ],
        input_output_aliases=aliases,
        compiler_params=_params("arbitrary", "arbitrary"),
        name="conv_ffn",
    )(*args)


def _sample_attn_kernel(sink_ref, x_ref, wq_ref, wkv_ref, wo_ref, cos_ref, sin_ref, ck_ref, cv_ref,
                        g_ref, b_ref, *rest, bb):
    o_ref, kw_ref, vw_ref, q_scr, kv_scr, o_scr = rest[-6:]
    i = pl.program_id(0)
    last = pl.num_programs(0) - 1
    n = x_ref.shape[0]
    w = WINDOW

    @pl.when(i == 0)
    def _():
        xb = x_ref[...].astype(BF16)
        lane = lax.broadcasted_iota(jnp.int32, (n, V7X_LANES), 1)
        first_half = (lane & 63) < 32
        cos = cos_ref[...]
        sin = sin_ref[...]
        zq = jnp.dot(xb, wq_ref[...], preferred_element_type=F32)
        for c in range(NQ // 128):
            q_scr[:, c * 128:(c + 1) * 128] = _rope_128(
                zq[:, c * 128:(c + 1) * 128], cos, sin, first_half) * ATTN_SCALE
        zkv = jnp.dot(xb, wkv_ref[...], preferred_element_type=F32)
        for c in range(NKV // 128):
            kv_scr[:, c * 128:(c + 1) * 128] = _rope_128(
                zkv[:, c * 128:(c + 1) * 128], cos, sin, first_half)
        kv_scr[:, NKV:] = zkv[:, NKV:]

    r16 = lax.broadcasted_iota(jnp.int32, (ATTN_HEADS, NKV), 0)
    l16 = lax.broadcasted_iota(jnp.int32, (ATTN_HEADS, NKV), 1)
    diag = (l16 >> 6) == (r16 & (ATTN_KV_HEADS - 1))
    rowkey = lax.broadcasted_iota(jnp.int32, (w, NKV), 0)
    sink = sink_ref[...]

    def token_group(gi, carry):
        toks = range(group)
        bis = [gi * group + u for u in toks]
        ts = [i * bb + bi for bi in bis]
        knew = [kv_scr[pl.ds(t, 1), 0:NKV] for t in ts]
        vnew = [kv_scr[pl.ds(t, 1), NKV:] for t in ts]
        qbd, s = [], []
        for u in toks:
            qrow = q_scr[pl.ds(ts[u], 1), :]
            qrep = jnp.broadcast_to(qrow[:, 3 * NKV:], (ATTN_HEADS, NKV))
            for hh in (2, 1, 0):
                qrep = jnp.where(r16 < (hh + 1) * ATTN_KV_HEADS,
                                 jnp.broadcast_to(qrow[:, hh * NKV:(hh + 1) * NKV], (ATTN_HEADS, NKV)),
                                 qrep)
            qbd.append(jnp.where(diag, qrep, 0.0))
            s.append(lax.dot_general(qbd[u].astype(BF16), ck_ref[bis[u]].astype(BF16),
                                     (((1,), (1,)), ((), ())), preferred_element_type=F32))
        for u in toks:
            kc = ck_ref[bis[u]]
            vc = cv_ref[bis[u]]
            kw_ref[bis[u]] = jnp.where(rowkey == w - 1, knew[u], pltpu.roll(kc, w - 1, 0))
            vw_ref[bis[u]] = jnp.where(rowkey == w - 1, vnew[u], pltpu.roll(vc, w - 1, 0))
        e, e_new, denom = [], [], []
        for u in toks:
            s_new = jnp.sum(qbd[u] * knew[u], axis=-1, keepdims=True)
            m = jnp.maximum(jnp.maximum(jnp.max(s[u], axis=-1, keepdims=True), s_new), sink)
            e.append(jnp.exp(s[u] - m))
            e_new.append(jnp.exp(s_new - m))
            denom.append(jnp.sum(e[u], axis=-1, keepdims=True) + e_new[u] + jnp.exp(sink - m))
        pv = [jnp.dot(e[u].astype(BF16), cv_ref[bis[u]].astype(BF16), preferred_element_type=F32)
              for u in toks]
        for u in toks:
            o = jnp.where(diag, (pv[u] + e_new[u] * vnew[u]) / denom[u], 0.0)
            for hh in range(ATTN_GROUP):
                part = jnp.where((r16 >> 2) == hh, o, 0.0)
                o_scr[pl.ds(ts[u], 1), hh * NKV:(hh + 1) * NKV] = jnp.sum(part, axis=0, keepdims=True)
        return carry

    group = min(8, bb)
    lax.fori_loop(0, bb // group, token_group, 0)

    @pl.when(i == last)
    def _():
        out = jnp.dot(o_scr[...].astype(BF16), wo_ref[...], preferred_element_type=F32)
        o_ref[...] = _deepnorm_ln(x_ref[...], out, g_ref[...], b_ref[...])


def _sample_attn(sinks_perm, x, wq_perm, wkv, wo_perm, cos, sin, cache_k_all, cache_v_all, g, bta,
                 prev_kv, *, layer, bb):
    n = x.shape[0]
    grid = (n // bb,)
    cache = pl.BlockSpec((None, bb, WINDOW, NKV), lambda i: (layer, i, 0, 0))
    args = [sinks_perm, x, wq_perm, wkv, wo_perm, cos, sin, cache_k_all, cache_v_all, g, bta]
    prev = list(prev_kv)
    prev_specs, aliases = _stacked_update(layer, prev, 1, len(args))
    return pl.pallas_call(
        functools.partial(_sample_attn_kernel, bb=bb),
        grid=grid,
        in_specs=[
            _full(sinks_perm.shape),
            _full(x.shape), _full(wq_perm.shape), _full(wkv.shape), _full(wo_perm.shape),
            _full(cos.shape), _full(sin.shape), cache, cache, _full(g.shape), _full(bta.shape),
        ] + prev_specs,
        out_specs=[_full((n, D_MODEL)), cache, cache],
        out_shape=[
            jax.ShapeDtypeStruct((n, D_MODEL), F32),
            jax.ShapeDtypeStruct(cache_k_all.shape, F32),
            jax.ShapeDtypeStruct(cache_v_all.shape, F32),
        ],
        scratch_shapes=[
            pltpu.VMEM((n, NQ), F32),
            pltpu.VMEM((n, 2 * NKV), F32),
            pltpu.VMEM((n, NQ), F32),
        ],
        input_output_aliases=aliases,
        compiler_params=_params("arbitrary"),
        name="sample_attn",
    )(*args, *prev)


def _sample_ret_in_kernel(x_ref, w_ref, cos_ref, sin_ref, qt_ref, kt_ref, v_ref, sg_ref):
    n = x_ref.shape[0]
    xb = x_ref[...].astype(BF16)
    lane = lax.broadcasted_iota(jnp.int32, (n, V7X_LANES), 1)
    even = (lane & 1) == 0
    for sec, out_ref, scale in ((0, qt_ref, None), (1, kt_ref, RET_K_SCALE)):
        z = jnp.dot(xb, w_ref[:, sec * RET_NQK:(sec + 1) * RET_NQK], preferred_element_type=F32)
        for c in range(RET_NQK // 128):
            t0 = (c % 2) * 128
            zc = _pair_rot_128(z[:, c * 128:(c + 1) * 128], cos_ref[:, t0:t0 + 128],
                               sin_ref[:, t0:t0 + 128], even)
            if scale is not None:
                zc = zc * scale
            out_ref[c * 128:(c + 1) * 128, :] = zc.T
    base = 2 * RET_NQK
    v_ref[...] = jnp.dot(xb, w_ref[:, base:base + RET_NV], preferred_element_type=F32)
    base = 2 * RET_NQK + RET_NV
    sg_ref[...] = _silu(jnp.dot(xb, w_ref[:, base:base + RET_NV], preferred_element_type=F32))


def _sample_ret_in(x, w_in_bf16, cos, sin):
    n = x.shape[0]
    return pl.pallas_call(
        _sample_ret_in_kernel,
        grid=(1,),
        in_specs=[_full(x.shape), _spec(w_in_bf16), _full(cos.shape), _full(sin.shape)],
        out_specs=[_full((RET_NQK, n)), _full((RET_NQK, n)), _full((n, RET_NV)), _full((n, RET_NV))],
        out_shape=[
            jax.ShapeDtypeStruct((RET_NQK, n), F32),
            jax.ShapeDtypeStruct((RET_NQK, n), F32),
            jax.ShapeDtypeStruct((n, RET_NV), F32),
            jax.ShapeDtypeStruct((n, RET_NV), F32),
        ],
        compiler_params=_params("arbitrary"),
        name="sample_ret_in",
    )(x, _arr(w_in_bf16), cos, sin)


def _stacked_update(layer, prev_outs, first_out_index, n_inputs):
    specs = [pl.BlockSpec(memory_space=pl.ANY) for _ in prev_outs]
    aliases = {n_inputs + k: first_out_index + k for k in range(len(prev_outs))}
    return specs, aliases


def _sample_ret_out_kernel(o_ref, sg_ref, x_ref, wo_ref, g_ref, b_ref, y_ref):
    n = x_ref.shape[0]
    acc = jnp.zeros((n, D_MODEL), F32)
    for h in range(RET_HEADS):
        vcols = slice(h * RET_VALUE_DIM, (h + 1) * RET_VALUE_DIM)
        o = o_ref[:, vcols]
        mu = jnp.mean(o, axis=-1, keepdims=True)
        d = o - mu
        var = jnp.mean(d * d, axis=-1, keepdims=True)
        gated = (sg_ref[:, vcols] * (d * lax.rsqrt(var + GN_EPS))).astype(BF16)
        acc = acc + jnp.dot(gated, wo_ref[vcols, :], preferred_element_type=F32)
    y_ref[...] = _deepnorm_ln(x_ref[...], acc, g_ref[...], b_ref[...])


def _sample_ret_out(o, sg, x, wo_bf16, g, bta):
    args = (o, sg, x, wo_bf16, g, bta)
    return pl.pallas_call(
        _sample_ret_out_kernel,
        grid=(1,),
        in_specs=[_spec(a) for a in args],
        out_specs=_full(x.shape),
        out_shape=jax.ShapeDtypeStruct(x.shape, F32),
        compiler_params=_params("arbitrary"),
        name="sample_ret_out",
    )(*map(_arr, args))


def _sample_ffn_kernel(x_ref, b0_ref, b1_ref, win_ref, cw_ref, wout_ref, g_ref, b_ref, y_ref, a_ref):
    n = x_ref.shape[0]
    xb = x_ref[...].astype(BF16)
    acc = jnp.zeros((n, D_MODEL), F32)
    for j in range(FFN_NCHUNK):
        cols = slice(j * FFN_CHUNK, (j + 1) * FFN_CHUNK)
        gcols = slice(D_FF + j * FFN_CHUNK, D_FF + (j + 1) * FFN_CHUNK)
        a = jnp.dot(xb, win_ref[:, cols], preferred_element_type=F32)
        gt = jnp.dot(xb, win_ref[:, gcols], preferred_element_type=F32)
        h = _conv_gate(a, gt, b1_ref[:, cols], b0_ref[:, cols], cw_ref[:, cols])
        a_ref[:, cols] = a
        acc = acc + jnp.dot(h.astype(BF16), wout_ref[cols, :], preferred_element_type=F32)
    y_ref[...] = _deepnorm_ln(x_ref[...], acc, g_ref[...], b_ref[...])


def _sample_ffn(x, buf0, buf1, win, cw, wout, g, bta):
    args = (x, buf0, buf1, win, cw, wout, g, bta)
    return pl.pallas_call(
        _sample_ffn_kernel,
        grid=(1,),
        in_specs=[_spec(a) for a in args],
        out_specs=[_full(x.shape), _full(buf0.shape)],
        out_shape=[jax.ShapeDtypeStruct(x.shape, F32), jax.ShapeDtypeStruct(buf0.shape, F32)],
        compiler_params=_params("arbitrary"),
        name="sample_ffn",
    )(*map(_arr, args))


def _head_major_to_group_minor(w, axis):
    shape = w.shape
    split = shape[:axis] + (ATTN_KV_HEADS, ATTN_GROUP, ATTN_HEAD_DIM) + shape[axis + 1:]
    return jnp.swapaxes(w.reshape(split), axis, axis + 1).reshape(shape)


def kernel(x_prompt, x_sample, cache_k_win, cache_v_win, state_ret, state_conv, attn_w_qkv, attn_sinks, attn_w_o, ret_w_in, ret_w_o, ffn_w_in, ffn_conv_w, ffn_conv_b, ffn_w_out, ln_mix_g, ln_mix_b, ln_ffn_g, ln_ffn_b):
    bp, tp, _ = x_prompt.shape
    ns = x_sample.shape[0]
    assert x_sample.shape[1] == 1, "the sample group carries one new token per sequence"
    xp = x_prompt
    xs = x_sample.reshape(ns, D_MODEL)
    pos_p = jnp.arange(tp)
    pos_s = jnp.full((ns,), PAST_LEN, jnp.int32)
    rope_p = _rope_tables(pos_p)
    rope_s = _rope_tables(pos_s)
    rot_p = _ret_rot_tables(pos_p)
    rot_s = _ret_rot_tables(pos_s)
    decay_in, q_decay, k_decay, chunk_decay = _ret_decay_tables(RET_CHUNK)
    gamma = jnp.exp(_ret_log_gamma())
    row = lambda v: v.reshape(1, D_MODEL)

    tm = min(512, tp)
    cache_k_all = cache_k_win.reshape(cache_k_win.shape[0], ns, WINDOW, NKV)
    cache_v_all = cache_v_win.reshape(cache_v_win.shape[0], ns, WINDOW, NKV)
    kv_s = ()
    rs = None
    kp_l, vp_l, rp_l, cp_l, cs_l = [], [], [], [], []
    attn_w_qkv_bf, attn_w_o_bf = attn_w_qkv.astype(BF16), attn_w_o.astype(BF16)
    ret_w_in_bf, ret_w_o_bf = ret_w_in.astype(BF16), ret_w_o.astype(BF16)
    ffn_w_in_bf, ffn_w_out_bf = ffn_w_in.astype(BF16), ffn_w_out.astype(BF16)
    for i in range(DEPTH):
        j = i // N_MIXERS
        g_mix, b_mix = row(ln_mix_g[i]), row(ln_mix_b[i])
        if i % N_MIXERS == 0:
            w_qkv = _LayerOf(attn_w_qkv_bf, j)
            w_o = _LayerOf(attn_w_o_bf, j)
            q, k2, v2, kf, vf = _attn_qkv(xp, w_qkv, *rope_p, tm=tm)
            xp = _attn_core(attn_sinks[j], q, k2, v2, xp, w_o, g_mix, b_mix, tq=tm)
            kp_l.append(kf.reshape(bp, WINDOW, ATTN_KV_HEADS, ATTN_HEAD_DIM))
            vp_l.append(vf.reshape(bp, WINDOW, ATTN_KV_HEADS, ATTN_HEAD_DIM))
            xs, *kv_s = _sample_attn(
                attn_sinks[j].reshape(ATTN_KV_HEADS, ATTN_GROUP).T.reshape(ATTN_HEADS, 1), xs,
                _head_major_to_group_minor(attn_w_qkv_bf[j, :, :NQ], 1), attn_w_qkv_bf[j, :, NQ:],
                _head_major_to_group_minor(attn_w_o_bf[j], 0), *rope_s, cache_k_all, cache_v_all,
                g_mix, b_mix, kv_s, layer=j, bb=min(16, ns))
        else:
            w_in = _LayerOf(ret_w_in_bf, j)
            w_o = _LayerOf(ret_w_o_bf, j)
            q, k, v, sg = _ret_in(xp, w_in, *rot_p, tm=tm)
            qt, kt, vs, sgs = _sample_ret_in(xs, w_in, *rot_s)
            half = ns // 2
            rider = _RetStateRider(gamma, qt, kt, vs, state_ret, rs, j, 0, half)
            xp, rp, os_a, rs = _ret_core(chunk_decay, q, k, v, sg, xp, decay_in, q_decay, k_decay,
                                         w_o, g_mix, b_mix, rider, ct=tm)
            rp_l.append(rp)
            ffn_rider = _RetStateRider(gamma, qt, kt, vs, state_ret, rs, j, half, ns - half)
        w_in = _LayerOf(ffn_w_in_bf, i)
        wout = _LayerOf(ffn_w_out_bf, i)
        cw = jnp.concatenate(
            [ffn_conv_w[i], ffn_conv_b[i][None], jnp.zeros((V7X_SUBLANES - CONV_WIDTH - 1, D_FF), F32)], axis=0)
        g_ffn, b_ffn = row(ln_ffn_g[i]), row(ln_ffn_b[i])
        if i % N_MIXERS == 0:
            xp, cp = _ffn(xp, w_in, cw, wout, g_ffn, b_ffn, None, tm=tm)
        else:
            xp, cp, os_b, rs = _ffn(xp, w_in, cw, wout, g_ffn, b_ffn, ffn_rider, tm=tm)
            xs = _sample_ret_out(jnp.concatenate([os_a, os_b], axis=0), sgs, xs, w_o, g_mix, b_mix)
        cp_l.append(cp)
        buf = state_conv[i]
        xs, a_new = _sample_ffn(xs, buf[:, 0, :], buf[:, 1, :], w_in, cw, wout, g_ffn, b_ffn)
        cs_l.append(jnp.stack([buf[:, 1, :], a_new], axis=1))
    ks, vs = (w.reshape(cache_k_win.shape) for w in kv_s)
    return (xp, xs.reshape(ns, 1, D_MODEL),
            jnp.stack(kp_l), jnp.stack(vp_l), jnp.stack(rp_l), jnp.stack(cp_l),
            ks, vs, rs, jnp.stack(cs_l))
```
